```python
import jax, jax.numpy as jnp
from jax import lax
import numpy as np

D_MODEL = 1024
BATCH = 8
SEQ = 8192
DEPTH = 4
DEC_BATCH = 16
DEC_SEQ = 64
PAST_LEN = 4096

CHUNK = 64
EPS = 1e-6
HG_HEADS = 4
HG_DK = 128
HG_DV = 128
HG_W = HG_HEADS * HG_DV
HG_CHUNK = 16
SSM_INNER = 1024
SSM_P = 64
SSM_HEADS = SSM_INNER // SSM_P
SSM_GROUPS = 2
SSM_N = 128
CONV_W = 4
CONV_DIM = SSM_INNER + 2 * SSM_GROUPS * SSM_N
SB_HEADS = 4
SB_DH = 128
SB_W = SB_HEADS * SB_DH
SB_BLOCK = 128
SB_KBLOCK = 128
D_MIX = HG_W + SSM_INNER + SB_W
HG_COLS = 4 * HG_W
SSM_COLS = SSM_INNER + CONV_DIM + SSM_HEADS
SB_COLS = 4 * SB_W
D_IN = HG_COLS + SSM_COLS + SB_COLS

kernel_name = 'hymba_style_hgrn2_ssd_stickbreak_stream_step'


def rms_norm(x, w):
    x32 = x.astype(jnp.float32)
    return x32 * lax.rsqrt(jnp.mean(x32 * x32, axis=-1, keepdims=True) + EPS) * w.astype(jnp.float32)


def hgrn_lower_bounds(lb_logits):
    p = jax.nn.softmax(lb_logits.astype(jnp.float32), axis=0)
    return jnp.clip(jnp.cumsum(p, axis=0) - p[0:1], 0.0, 1.0)


def pad_time(a, pad):
    if pad == 0:
        return a
    return jnp.pad(a, ((0, 0), (0, pad)) + ((0, 0),) * (a.ndim - 2))


def to_chunks(a, L):
    b, t = a.shape[:2]
    return jnp.moveaxis(a.reshape((b, t // L, L) + a.shape[2:]), 1, 0)


def from_chunks(a):
    a = jnp.moveaxis(a, 0, 1)
    return a.reshape((a.shape[0], a.shape[1] * a.shape[2]) + a.shape[3:])


def hgrn2_scan(q, k, v, log_f, s0):
    T = q.shape[1]
    L = min(HG_CHUNK, T)
    pad = (-T) % L
    q, k, v, log_f = (pad_time(a, pad) for a in (q, k, v, log_f))
    causal = jnp.tril(jnp.ones((L, L), bool))

    def step(s, inp):
        qc, kc, vc, gc = inp
        b = jnp.cumsum(gc, axis=1)
        o_inter = jnp.einsum('blhk,bhkv->blhv', qc * jnp.exp(b), s)
        diff = b[:, :, None] - b[:, None, :]
        decay = jnp.exp(jnp.where(causal[None, :, :, None, None], diff, -jnp.inf))
        att = jnp.einsum('bthk,bshk,btshk->btsh', qc, kc, decay)
        o_intra = jnp.einsum('btsh,bshv->bthv', att, vc)
        b_last = b[:, -1]
        k_dec = kc * jnp.exp(b_last[:, None] - b)
        s_new = jnp.exp(b_last)[..., None] * s + jnp.einsum('bshk,bshv->bhkv', k_dec, vc)
        return s_new, o_inter + o_intra

    s_fin, o = lax.scan(step, s0, tuple(to_chunks(a, L) for a in (q, k, v, log_f)))
    return from_chunks(o)[:, :T], s_fin


def ssd_scan(x, dt, a, bm, cm, s0):
    bsz, T = x.shape[:2]
    L = min(CHUNK, T)
    pad = (-T) % L
    x, dt, bm, cm = (pad_time(t_, pad) for t_ in (x, dt, bm, cm))
    Tp = T + pad
    hpg = SSM_HEADS // SSM_GROUPS
    causal = jnp.tril(jnp.ones((L, L), bool))
    xg = x.reshape(bsz, Tp, SSM_GROUPS, hpg, SSM_P)
    dtg = dt.reshape(bsz, Tp, SSM_GROUPS, hpg)
    ag = a.reshape(SSM_GROUPS, hpg)
    sg0 = s0.reshape(bsz, SSM_GROUPS, hpg, SSM_P, SSM_N)

    def step(s, inp):
        xc, dtc, bc, cc = inp
        cs = jnp.cumsum(dtc * ag, axis=1)
        y_inter = jnp.einsum('blgn,bghpn->blghp', cc, s) * jnp.exp(cs)[..., None]
        seg = cs[:, :, None] - cs[:, None]
        decay = jnp.exp(jnp.where(causal[None, :, :, None, None], seg, -jnp.inf))
        cb = jnp.einsum('btgn,bsgn->btsg', cc, bc)
        w = cb[..., None] * decay * dtc[:, None]
        y_intra = jnp.einsum('btsgh,bsghp->btghp', w, xc)
        cs_last = cs[:, -1]
        wdec = jnp.exp(cs_last[:, None] - cs) * dtc
        s_new = jnp.exp(cs_last)[..., None, None] * s + jnp.einsum('blgh,blghp,blgn->bghpn', wdec, xc, bc)
        return s_new, y_inter + y_intra

    s_fin, y = lax.scan(step, sg0, tuple(to_chunks(t_, L) for t_ in (xg, dtg, bm, cm)))
    return (from_chunks(y).reshape(bsz, Tp, SSM_HEADS, SSM_P)[:, :T],
            s_fin.reshape(bsz, SSM_HEADS, SSM_P, SSM_N))


def causal_conv(xbc, prefix, w, b):
    T = xbc.shape[1]
    xp = jnp.concatenate([prefix.astype(jnp.float32), xbc], axis=1)
    out = b.astype(jnp.float32) + sum(w[j] * xp[:, j:j + T] for j in range(CONV_W))
    return jax.nn.silu(out), xp[:, T:]


def stick_breaking_block(q, q_pos, kb, vb, k_pos):
    z = jnp.einsum('bqhd,bnkhd->bhqnk', q, kb) * (SB_DH ** -0.5)
    mask = k_pos[None] < q_pos[:, None, None]
    ls = jnp.where(mask, jax.nn.log_sigmoid(-z), 0.0)
    tri = jnp.tril(jnp.ones((SB_KBLOCK, SB_KBLOCK), ls.dtype))
    within = jnp.einsum('bhqnk,kj->bhqnj', ls, tri)
    blk_tot = within[..., 0]
    later = lax.cumsum(blk_tot, axis=3, reverse=True) - blk_tot
    w = jnp.where(mask, jnp.exp(z + within + later[..., None]), 0.0)
    return jnp.einsum('bhqnk,bnkhd->bqhd', w, vb)


def stick_breaking(q, k, v):
    bsz, T = q.shape[:2]
    K = k.shape[1]
    nkb_all = -(-K // SB_KBLOCK)
    kpad = nkb_all * SB_KBLOCK - K
    kb = pad_time(k, kpad).reshape(bsz, nkb_all, SB_KBLOCK, SB_HEADS, SB_DH)
    vb = pad_time(v, kpad).reshape(bsz, nkb_all, SB_KBLOCK, SB_HEADS, SB_DH)
    k_pos = jnp.arange(nkb_all * SB_KBLOCK).reshape(nkb_all, SB_KBLOCK)
    outs = []
    for start in range(0, T, SB_BLOCK):
        stop = min(start + SB_BLOCK, T)
        q_pos = (K - T) + jnp.arange(start, stop)
        nkb = max(1, -(-((K - T) + stop - 1) // SB_KBLOCK))
        outs.append(stick_breaking_block(q[:, start:stop], q_pos, kb[:, :nkb], vb[:, :nkb], k_pos[:nkb]))
    return jnp.concatenate(outs, axis=1)


def trunk_layer(x, conv_prev, ssm_prev, hg_prev, k_prev, v_prev,
                lb, norm_w, w_in, conv_w, conv_b, dt_bias, a_log, d_skip, ssm_norm_w, hg_norm_w, w_out):
    bsz, T, _ = x.shape
    f32 = jnp.float32
    h = rms_norm(x, norm_w)
    proj = jnp.einsum('btd,de->bte', h, w_in.astype(f32))
    p_hg, p_ssm, p_sb = jnp.split(proj, [HG_COLS, HG_COLS + SSM_COLS], axis=-1)

    hq, hf, hi, hgate = jnp.split(p_hg, 4, axis=-1)
    lb = lb.astype(f32)
    log_f = jnp.logaddexp(jnp.log(lb), jnp.log1p(-lb) + jax.nn.log_sigmoid(hf))
    hk = (1.0 - lb) * jax.nn.sigmoid(-hf)
    heads = (bsz, T, HG_HEADS, HG_DK)
    o_hg, hg_state = hgrn2_scan(jax.nn.silu(hq).reshape(heads), hk.reshape(heads),
                                hi.reshape(bsz, T, HG_HEADS, HG_DV), log_f.reshape(heads),
                                hg_prev.astype(f32))
    o_hg = rms_norm(o_hg, hg_norm_w).reshape(bsz, T, HG_W) * jax.nn.silu(hgate)

    z, xbc, dt_raw = jnp.split(p_ssm, [SSM_INNER, SSM_INNER + CONV_DIM], axis=-1)
    xbc, conv_state = causal_conv(xbc, conv_prev, conv_w.astype(f32), conv_b)
    xs, bm, cm = jnp.split(xbc, [SSM_INNER, SSM_INNER + SSM_GROUPS * SSM_N], axis=-1)
    dt = jax.nn.softplus(dt_raw + dt_bias.astype(f32))
    a = -jnp.exp(a_log.astype(f32))
    xs = xs.reshape(bsz, T, SSM_HEADS, SSM_P)
    y, ssm_state = ssd_scan(xs, dt, a, bm.reshape(bsz, T, SSM_GROUPS, SSM_N),
                            cm.reshape(bsz, T, SSM_GROUPS, SSM_N), ssm_prev.astype(f32))
    y = (y + d_skip.astype(f32)[:, None] * xs).reshape(bsz, T, SSM_INNER) * jax.nn.silu(z)
    gsz = SSM_INNER // SSM_GROUPS
    y = rms_norm(y.reshape(bsz, T, SSM_GROUPS, gsz), ssm_norm_w.reshape(SSM_GROUPS, gsz)).reshape(bsz, T, SSM_INNER)

    sq, sk, sv, sgate = jnp.split(p_sb, 4, axis=-1)
    sh = (bsz, T, SB_HEADS, SB_DH)
    sq, sk, sv = sq.reshape(sh), sk.reshape(sh), sv.reshape(sh)
    k_all = jnp.concatenate([k_prev.astype(f32), sk], axis=1)
    v_all = jnp.concatenate([v_prev.astype(f32), sv], axis=1)
    o_sb = stick_breaking(sq, k_all, v_all).reshape(bsz, T, SB_W) * jax.nn.silu(sgate)

    mixed = jnp.concatenate([o_hg, y, o_sb], axis=-1)
    x = x + jnp.einsum('bte,ed->btd', mixed, w_out.astype(f32)).astype(x.dtype)
    return x, (hg_state, ssm_state, conv_state, sk, sv)


def setup_inputs(seed: int = 0) -> dict:
    key = jax.random.key(seed)
    ks = jax.random.split(key, 20)
    f32 = jnp.float32

    def nrm(k, shape, scale):
        return scale * jax.random.normal(k, shape, f32)

    dt0 = jnp.exp(jax.random.uniform(ks[12], (DEPTH, SSM_HEADS), f32, np.log(1e-3), np.log(1e-1)))
    return {
        'x_prompt': nrm(ks[0], (BATCH, SEQ, D_MODEL), 1.0),
        'x_sample': nrm(ks[1], (DEC_BATCH, DEC_SEQ, D_MODEL), 1.0),
        'state_hgrn': nrm(ks[2], (DEPTH, DEC_BATCH, HG_HEADS, HG_DK, HG_DV), 0.5),
        'state_ssm': nrm(ks[3], (DEPTH, DEC_BATCH, SSM_HEADS, SSM_P, SSM_N), 0.1),
        'state_conv': nrm(ks[4], (DEPTH, DEC_BATCH, CONV_W - 1, CONV_DIM), 1.0),
        'cache_k': nrm(ks[5], (DEPTH, DEC_BATCH, PAST_LEN, SB_HEADS, SB_DH), 1.0),
        'cache_v': nrm(ks[6], (DEPTH, DEC_BATCH, PAST_LEN, SB_HEADS, SB_DH), 1.0),
        'norm_w': 1.0 + nrm(ks[7], (DEPTH, D_MODEL), 0.02),
        'w_in': nrm(ks[8], (DEPTH, D_MODEL, D_IN), D_MODEL ** -0.5),
        'lb_logits': nrm(ks[9], (DEPTH, HG_HEADS * HG_DK), 0.5),
        'conv_w': nrm(ks[10], (DEPTH, CONV_W, CONV_DIM), CONV_W ** -0.5),
        'conv_b': nrm(ks[11], (DEPTH, CONV_DIM), 0.02),
        'dt_bias': dt0 + jnp.log(-jnp.expm1(-dt0)),
        'a_log': jnp.log(jax.random.uniform(ks[13], (DEPTH, SSM_HEADS), f32, 1.0, 16.0)),
        'd_skip': 1.0 + nrm(ks[14], (DEPTH, SSM_HEADS), 0.02),
        'ssm_norm_w': 1.0 + nrm(ks[15], (DEPTH, SSM_INNER), 0.02),
        'hg_norm_w': 1.0 + nrm(ks[16], (DEPTH, HG_DV), 0.02),
        'w_out': nrm(ks[17], (DEPTH, D_MIX, D_MODEL), D_MIX ** -0.5),
        'final_norm_w': 1.0 + nrm(ks[18], (D_MODEL,), 0.02),
    }


def stack_states(states):
    return tuple(jnp.stack([st[i] for st in states]) for i in range(5))


def reference(x_prompt, x_sample, state_hgrn, state_ssm, state_conv, cache_k, cache_v,
              norm_w, w_in, lb_logits, conv_w, conv_b, dt_bias, a_log, d_skip,
              ssm_norm_w, hg_norm_w, w_out, final_norm_w):
    f32 = jnp.float32
    lbs = hgrn_lower_bounds(lb_logits)

    def layer_weights(l):
        return (lbs[l], norm_w[l], w_in[l], conv_w[l], conv_b[l], dt_bias[l], a_log[l],
                d_skip[l], ssm_norm_w[l], hg_norm_w[l], w_out[l])

    yp = x_prompt
    p_states = []
    for l in range(DEPTH):
        yp, st = trunk_layer(yp,
                             jnp.zeros((BATCH, CONV_W - 1, CONV_DIM), f32),
                             jnp.zeros((BATCH, SSM_HEADS, SSM_P, SSM_N), f32),
                             jnp.zeros((BATCH, HG_HEADS, HG_DK, HG_DV), f32),
                             jnp.zeros((BATCH, 0, SB_HEADS, SB_DH), f32),
                             jnp.zeros((BATCH, 0, SB_HEADS, SB_DH), f32),
                             *layer_weights(l))
        p_states.append(st)
    y_prompt = rms_norm(yp, final_norm_w).astype(x_prompt.dtype)

    ys = x_sample
    s_states = []
    for l in range(DEPTH):
        ys, st = trunk_layer(ys, state_conv[l], state_ssm[l], state_hgrn[l], cache_k[l], cache_v[l],
                             *layer_weights(l))
        s_states.append(st)
    y_sample = rms_norm(ys, final_norm_w).astype(x_sample.dtype)

    p_hg, p_ssm, p_conv, p_k, p_v = stack_states(p_states)
    s_hg, s_ssm, s_conv, s_k, s_v = stack_states(s_states)
    return (y_prompt, y_sample, p_hg, p_ssm, p_conv, p_k, p_v, s_hg, s_ssm, s_conv, s_k, s_v)
```

```python
import functools

import jax
import jax.numpy as jnp
from jax import lax
from jax.experimental import pallas as pl
from jax.experimental.pallas import tpu as pltpu

F32 = jnp.float32
BF16 = jnp.bfloat16
EPS = 1e-6

LANE = 128
HG_HEADS = 4
HG_D = 128
SSM_HEADS = 16
SSM_P = 64
SSM_N = 128
SSM_GROUPS = 2
SSM_INNER = SSM_HEADS * SSM_P
CONV_W = 4
CONV_DIM = SSM_INNER + 2 * SSM_GROUPS * SSM_N
SB_HEADS = 4
SB_DH = 128
HG_W = HG_HEADS * HG_D
SB_W = SB_HEADS * SB_DH

NB_HG_Q, NB_HG_F, NB_HG_I, NB_HG_G = 0, 4, 8, 12
NB_Z = 16
NB_XBC = 24
NB_SB_Q, NB_SB_K, NB_SB_V, NB_SB_G = 36, 40, 44, 48
NB_DT = 52
NP_BLOCKS = 54
NP = NP_BLOCKS * LANE
PROJ_TN = 9 * LANE

VMEM_LIMIT = 56 * 1024 * 1024

SSD_PAD = 8


def _sigmoid(x):
    return 1.0 / (1.0 + jnp.exp(-x))


def _log_sigmoid(x):
    return jnp.minimum(x, 0.0) - jnp.log1p(jnp.exp(-jnp.abs(x)))


def _cumsum_rows(x):
    n = x.shape[0]
    row = lax.broadcasted_iota(jnp.int32, x.shape, 0)
    sh = 1
    while sh < n:
        x = x + jnp.where(row >= sh, pltpu.roll(x, sh, axis=0), 0.0)
        sh *= 2
    return x


def _transpose_rows(x):
    n = x.shape[0]
    if n < LANE:
        x = jnp.concatenate([x, jnp.zeros((LANE - n, LANE), x.dtype)], axis=0)
    return x.T[:, :n]


def _inproj_kernel(x_ref, nw_ref, w_ref, o_ref, xn_ref):
    @pl.when(pl.program_id(1) == 0)
    def _():
        x = x_ref[...]
        ms = jnp.mean(x * x, axis=-1, keepdims=True)
        xn_ref[...] = (x * lax.rsqrt(ms + EPS) * nw_ref[...]).astype(BF16)

    o_ref[...] = jnp.dot(xn_ref[...], w_ref[...], preferred_element_type=F32)


def _in_proj(x2d, norm_w, w_bf):
    m, d = x2d.shape
    tm = min(1024, m)
    return pl.pallas_call(
        _inproj_kernel,
        grid=(m // tm, NP // PROJ_TN),
        in_specs=[
            pl.BlockSpec((tm, d), lambda i, j: (i, 0)),
            pl.BlockSpec((1, d), lambda i, j: (0, 0)),
            pl.BlockSpec((d, PROJ_TN), lambda i, j: (0, j)),
        ],
        out_specs=pl.BlockSpec((tm, PROJ_TN), lambda i, j: (i, j)),
        out_shape=jax.ShapeDtypeStruct((m, NP), F32),
        scratch_shapes=[pltpu.VMEM((tm, d), BF16)],
        compiler_params=pltpu.CompilerParams(
            dimension_semantics=("parallel", "arbitrary"), vmem_limit_bytes=VMEM_LIMIT),
        name="in_proj",
    )(x2d, norm_w.reshape(1, d), w_bf)


def _hgrn_kernel(*refs, C, has_state):
    if has_state:
        q_ref, f_ref, i_ref, g_ref, lb_ref, nw_ref, s0_ref, o_ref, st_ref, sT_ref, b_ref = refs
    else:
        q_ref, f_ref, i_ref, g_ref, lb_ref, nw_ref, o_ref, st_ref, sT_ref, b_ref = refs
    c = pl.program_id(2)

    @pl.when(c == 0)
    def _():
        if has_state:
            sT_ref[...] = s0_ref[0, 0].T
        else:
            sT_ref[...] = jnp.zeros((HG_D, HG_D), F32)

    hq = q_ref[0]
    hf = f_ref[0]
    v = i_ref[0]
    gate = g_ref[0]
    lb = lb_ref[...]

    q = hq * _sigmoid(hq)
    e = jnp.exp(-jnp.abs(hf))
    la = jnp.log(lb)
    lc = jnp.log1p(-lb) + (jnp.minimum(hf, 0.0) - jnp.log1p(e))
    log_f = jnp.maximum(la, lc) + jnp.log1p(jnp.exp(-jnp.abs(la - lc)))
    k = (1.0 - lb) * (jnp.where(hf >= 0.0, e, 1.0) / (1.0 + e))

    b = _cumsum_rows(log_f)
    b_ref[...] = b
    row = lax.broadcasted_iota(jnp.int32, (C, HG_D), 0)

    att = jnp.zeros((C, C), F32)
    n = C
    while n >= 16:
        h = n // 2
        pieces = [jnp.broadcast_to(b_ref[blk * n + h - 1:blk * n + h, :], (n, HG_D)) for blk in range(C // n)]
        b_mid = pieces[0] if len(pieces) == 1 else jnp.concatenate(pieces, axis=0)
        second = (row & (n - 1)) >= h
        ex = jnp.exp(jnp.where(second, b - b_mid, b_mid - b))
        q_n = jnp.where(second, q * ex, 0.0).astype(BF16)
        k_n = jnp.where(second, 0.0, k * ex).astype(BF16)
        a_n = lax.dot_general(q_n, k_n, (((1,), (1,)), ((), ())), preferred_element_type=F32)
        if n < C:
            shift = n.bit_length() - 1
            rb = lax.broadcasted_iota(jnp.int32, (C, C), 0) >> shift
            cb = lax.broadcasted_iota(jnp.int32, (C, C), 1) >> shift
            a_n = jnp.where(rb == cb, a_n, 0.0)
        att = att + a_n
        n = h

    ones = jnp.ones((HG_D, HG_D), BF16)
    r8 = row & 7
    o = jnp.zeros((C, HG_D), F32)
    for d in range(8):
        if d == 0:
            p = q * k
            vs = v
        else:
            ks = pltpu.roll(k, d, axis=0)
            bs = pltpu.roll(b, d, axis=0)
            vs = pltpu.roll(v, d, axis=0)
            p = jnp.where(r8 >= d, q * ks * jnp.exp(b - bs), 0.0)
        a_d = jnp.dot(p.astype(BF16), ones, preferred_element_type=F32)
        o = o + a_d * vs

    s_t = sT_ref[...]
    v_bf = v.astype(BF16)
    o = o + lax.dot_general((q * jnp.exp(b)).astype(BF16), s_t.astype(BF16),
                            (((1,), (1,)), ((), ())), preferred_element_type=F32)
    o = o + jnp.dot(att.astype(BF16), v_bf, preferred_element_type=F32)

    b_last = b_ref[C - 1:C, :]
    k_dec = (k * jnp.exp(b_last - b)).astype(BF16)
    upd = lax.dot_general(v_bf, k_dec, (((0,), (0,)), ((), ())), preferred_element_type=F32)
    sT_ref[...] = s_t * jnp.exp(b_last) + upd

    ms = jnp.mean(o * o, axis=-1, keepdims=True)
    o = o * lax.rsqrt(ms + EPS) * nw_ref[...]
    o_ref[0] = (o * (gate * _sigmoid(gate))).astype(o_ref.dtype)

    @pl.when(c == pl.num_programs(2) - 1)
    def _():
        st_ref[0, 0] = sT_ref[...].T


def _hgrn(proj, lb, nw, s0):
    bsz, t, _ = proj.shape
    C = min(128, t)
    has_state = s0 is not None

    def col(nb):
        return pl.BlockSpec((1, C, HG_D), lambda b, h, c: (b, c, nb + h))

    in_specs = [col(NB_HG_Q), col(NB_HG_F), col(NB_HG_I), col(NB_HG_G),
                pl.BlockSpec((1, HG_D), lambda b, h, c: (0, h)),
                pl.BlockSpec((1, HG_D), lambda b, h, c: (0, 0))]
    args = [proj, proj, proj, proj, lb.reshape(1, HG_W), nw.reshape(1, HG_D)]
    if has_state:
        in_specs.append(pl.BlockSpec((1, 1, HG_D, HG_D), lambda b, h, c: (b, h, 0, 0)))
        args.append(s0)
    return pl.pallas_call(
        functools.partial(_hgrn_kernel, C=C, has_state=has_state),
        grid=(bsz, HG_HEADS, t // C),
        in_specs=in_specs,
        out_specs=[pl.BlockSpec((1, C, HG_D), lambda b, h, c: (b, c, h)),
                   pl.BlockSpec((1, 1, HG_D, HG_D), lambda b, h, c: (b, h, 0, 0))],
        out_shape=[jax.ShapeDtypeStruct((bsz, t, HG_W), BF16),
                   jax.ShapeDtypeStruct((bsz, HG_HEADS, HG_D, HG_D), F32)],
        scratch_shapes=[pltpu.VMEM((HG_D, HG_D), F32), pltpu.VMEM((C, HG_D), F32)],
        compiler_params=pltpu.CompilerParams(
            dimension_semantics=("parallel", "parallel", "arbitrary"), vmem_limit_bytes=VMEM_LIMIT),
        name="hgrn2_scan",
    )(*args)


def _ssd_kernel(*refs, L, has_state):
    if has_state:
        (z_ref, xbc_ref, dt_ref, cw_ref, cb_ref, dtb_ref, al_ref, dsk_ref, nw_ref, conv0_ref, s0_ref,
         y_ref, st_ref, cst_ref, xp_ref, s_ref, ys_ref) = refs
    else:
        (z_ref, xbc_ref, dt_ref, cw_ref, cb_ref, dtb_ref, al_ref, dsk_ref, nw_ref,
         y_ref, st_ref, cst_ref, xp_ref, s_ref, ys_ref) = refs
    c = pl.program_id(1)

    @pl.when(c == 0)
    def _():
        xp_ref[0:SSD_PAD, :] = jnp.zeros((SSD_PAD, CONV_DIM), F32)
        if has_state:
            xp_ref[SSD_PAD - (CONV_W - 1):SSD_PAD, :] = conv0_ref[0]
            s_ref[...] = s0_ref[0]
        else:
            s_ref[...] = jnp.zeros((SSM_HEADS, SSM_P, SSM_N), F32)

    xp_ref[SSD_PAD:SSD_PAD + L, :] = xbc_ref[0]
    cw = cw_ref[...]
    conv = cb_ref[...]
    for j in range(CONV_W):
        lo = SSD_PAD - (CONV_W - 1) + j
        conv = conv + cw[j:j + 1, :] * xp_ref[lo:lo + L, :]
    cst_ref[0] = xp_ref[L + SSD_PAD - (CONV_W - 1):L + SSD_PAD, :]
    xp_ref[0:SSD_PAD, :] = xp_ref[L:L + SSD_PAD, :]
    act = conv * _sigmoid(conv)
    xs = act[:, :SSM_INNER]

    dt_raw = dt_ref[0] + dtb_ref[...]
    dt = jnp.maximum(dt_raw, 0.0) + jnp.log1p(jnp.exp(-jnp.abs(dt_raw)))
    a = -jnp.exp(al_ref[...])
    cs = _cumsum_rows(dt * a)
    cs_t = _transpose_rows(cs)
    dt_t = _transpose_rows(dt)

    tril = lax.broadcasted_iota(jnp.int32, (L, L), 0) >= lax.broadcasted_iota(jnp.int32, (L, L), 1)
    hpg = SSM_HEADS // SSM_GROUPS
    for g in range(SSM_GROUPS):
        b_g = act[:, SSM_INNER + g * SSM_N:SSM_INNER + (g + 1) * SSM_N].astype(BF16)
        c_g = act[:, SSM_INNER + (SSM_GROUPS + g) * SSM_N:SSM_INNER + (SSM_GROUPS + g + 1) * SSM_N].astype(BF16)
        cb = lax.dot_general(c_g, b_g, (((1,), (1,)), ((), ())), preferred_element_type=F32)
        for hh in range(hpg):
            h = g * hpg + hh
            cs_col = cs[:, h:h + 1]
            cs_row = cs_t[h:h + 1, :]
            decay = jnp.exp(jnp.where(tril, cs_col - cs_row, -jnp.inf))
            w = (cb * decay * dt_t[h:h + 1, :]).astype(BF16)
            x_h = xs[:, h * SSM_P:(h + 1) * SSM_P]
            s_h = s_ref[h]
            y_h = jnp.dot(w, x_h.astype(BF16), preferred_element_type=F32)
            y_h = y_h + jnp.exp(cs_col) * lax.dot_general(
                c_g, s_h.astype(BF16), (((1,), (1,)), ((), ())), preferred_element_type=F32)
            ys_ref[:, h * SSM_P:(h + 1) * SSM_P] = y_h
            cs_last = cs_t[h:h + 1, L - 1:L]
            wdec = jnp.exp(cs_last - cs_col) * dt[:, h:h + 1]
            upd = lax.dot_general((x_h * wdec).astype(BF16), b_g, (((0,), (0,)), ((), ())),
                                  preferred_element_type=F32)
            s_ref[h] = jnp.exp(cs_last) * s_h + upd

    z = z_ref[0]
    y = (ys_ref[...] + dsk_ref[...] * xs) * (z * _sigmoid(z))
    gsz = SSM_INNER // SSM_GROUPS
    nw = nw_ref[...]
    for g in range(SSM_GROUPS):
        yg = y[:, g * gsz:(g + 1) * gsz]
        ms = jnp.mean(yg * yg, axis=-1, keepdims=True)
        y_ref[0, :, g * gsz:(g + 1) * gsz] = (yg * lax.rsqrt(ms + EPS) * nw[:, g * gsz:(g + 1) * gsz]).astype(y_ref.dtype)

    @pl.when(c == pl.num_programs(1) - 1)
    def _():
        st_ref[0] = s_ref[...]


def _pad_lanes(v):
    return jnp.pad(v.astype(F32), (0, LANE - v.shape[0])).reshape(1, LANE)


def _ssd(proj, conv_w, conv_b, dt_bias, a_log, d_skip, ssm_norm_w, conv0, s0):
    bsz, t, _ = proj.shape
    L = min(128, t)
    has_state = s0 is not None

    def const(shape):
        return pl.BlockSpec(shape, lambda b, c: (0,) * len(shape))

    in_specs = [pl.BlockSpec((1, L, SSM_INNER), lambda b, c: (b, c, NB_Z * LANE // SSM_INNER)),
                pl.BlockSpec((1, L, CONV_DIM), lambda b, c: (b, c, NB_XBC * LANE // CONV_DIM)),
                pl.BlockSpec((1, L, LANE), lambda b, c: (b, c, NB_DT)),
                const((CONV_W, CONV_DIM)), const((1, CONV_DIM)), const((1, LANE)), const((1, LANE)),
                const((1, SSM_INNER)), const((1, SSM_INNER))]
    args = [proj, proj, proj, conv_w, conv_b.reshape(1, CONV_DIM), _pad_lanes(dt_bias), _pad_lanes(a_log),
            jnp.repeat(d_skip.astype(F32), SSM_P).reshape(1, SSM_INNER), ssm_norm_w.reshape(1, SSM_INNER)]
    if has_state:
        in_specs += [pl.BlockSpec((1, CONV_W - 1, CONV_DIM), lambda b, c: (b, 0, 0)),
                     pl.BlockSpec((1, SSM_HEADS, SSM_P, SSM_N), lambda b, c: (b, 0, 0, 0))]
        args += [conv0, s0]
    return pl.pallas_call(
        functools.partial(_ssd_kernel, L=L, has_state=has_state),
        grid=(bsz, t // L),
        in_specs=in_specs,
        out_specs=[pl.BlockSpec((1, L, SSM_INNER), lambda b, c: (b, c, 0)),
                   pl.BlockSpec((1, SSM_HEADS, SSM_P, SSM_N), lambda b, c: (b, 0, 0, 0)),
                   pl.BlockSpec((1, CONV_W - 1, CONV_DIM), lambda b, c: (b, 0, 0))],
        out_shape=[jax.ShapeDtypeStruct((bsz, t, SSM_INNER), BF16),
                   jax.ShapeDtypeStruct((bsz, SSM_HEADS, SSM_P, SSM_N), F32),
                   jax.ShapeDtypeStruct((bsz, CONV_W - 1, CONV_DIM), F32)],
        scratch_shapes=[pltpu.VMEM((L + SSD_PAD, CONV_DIM), F32),
                        pltpu.VMEM((SSM_HEADS, SSM_P, SSM_N), F32),
                        pltpu.VMEM((L, SSM_INNER), F32)],
        compiler_params=pltpu.CompilerParams(
            dimension_semantics=("parallel", "arbitrary"), vmem_limit_bytes=VMEM_LIMIT),
        name="conv_ssd_scan",
    )(*args)


def _tri_ext(kb):
    kk = lax.broadcasted_iota(jnp.int32, (kb, kb + LANE), 0)
    jj = lax.broadcasted_iota(jnp.int32, (kb, kb + LANE), 1)
    return jnp.where((kk >= jj) | (jj >= kb), 1.0, 0.0).astype(BF16)


def _sb_kernel(*refs, QB, KP, n_past):
    if n_past:
        q_ref, k_ref, v_ref, g_ref, kp_ref, vp_ref, o_ref, kb_ref, vb_ref, acc_ref, r_ref = refs
    else:
        q_ref, k_ref, v_ref, g_ref, o_ref, kb_ref, vb_ref, acc_ref, r_ref = refs
    qi = pl.program_id(2)

    @pl.when(qi == 0)
    def _():
        kb_ref[...] = k_ref[0].astype(BF16)
        vb_ref[...] = v_ref[0].astype(BF16)

    q = (q_ref[0] * (SB_DH ** -0.5)).astype(BF16)
    nt = (((1,), (1,)), ((), ()))

    start = pl.multiple_of(qi * QB, QB)
    mask = lax.broadcasted_iota(jnp.int32, (QB, QB), 1) < lax.broadcasted_iota(jnp.int32, (QB, QB), 0)
    z = lax.dot_general(q, kb_ref[pl.ds(start, QB), :], nt, preferred_element_type=F32)
    ls = jnp.where(mask, _log_sigmoid(-z), 0.0)
    wt = jnp.dot(ls.astype(BF16), _tri_ext(QB), preferred_element_type=F32)
    w = jnp.where(mask, jnp.exp(z + wt[:, :QB]), 0.0)
    acc_ref[...] = jnp.dot(w.astype(BF16), vb_ref[pl.ds(start, QB), :], preferred_element_type=F32)
    r_ref[...] = wt[:, QB:]

    def tile(k_t, v_t, tri):
        kb = k_t.shape[0]
        zt = lax.dot_general(q, k_t, nt, preferred_element_type=F32)
        wtt = jnp.dot(_log_sigmoid(-zt).astype(BF16), tri, preferred_element_type=F32)
        r = r_ref[...]
        wgt = jnp.exp(zt + wtt[:, :kb] + jnp.tile(r, (1, kb // LANE)))
        acc_ref[...] += jnp.dot(wgt.astype(BF16), v_t, preferred_element_type=F32)
        r_ref[...] = r + wtt[:, kb:]

    if QB % LANE == 0:
        tri_q = _tri_ext(QB)

        def new_body(jj, carry):
            s = pl.multiple_of((qi - 1 - jj) * QB, QB)
            tile(kb_ref[pl.ds(s, QB), :], vb_ref[pl.ds(s, QB), :], tri_q)
            return carry

        lax.fori_loop(0, qi, new_body, 0)

    if n_past:
        tri_p = _tri_ext(KP)

        def past_body(jj, carry):
            s = pl.multiple_of((n_past - 1 - jj) * KP, KP)
            tile(kp_ref[0, pl.ds(s, KP), :].astype(BF16), vp_ref[0, pl.ds(s, KP), :].astype(BF16), tri_p)
            return carry

        lax.fori_loop(0, n_past, past_body, 0)

    gate = g_ref[0]
    o_ref[0] = (acc_ref[...] * (gate * _sigmoid(gate))).astype(o_ref.dtype)


def _stick_breaking(proj, k_past, v_past):
    bsz, t, _ = proj.shape
    QB = min(256, t)
    KP = 256
    n_past = 0 if k_past is None else k_past.shape[1] // KP
    assert t % QB == 0 and (t == QB or QB % LANE == 0)

    def blk(nb):
        return pl.BlockSpec((1, QB, SB_DH), lambda b, h, i: (b, i, nb + h))

    def full(nb):
        return pl.BlockSpec((1, t, SB_DH), lambda b, h, i: (b, 0, nb + h))

    in_specs = [blk(NB_SB_Q), full(NB_SB_K), full(NB_SB_V), blk(NB_SB_G)]
    args = [proj, proj, proj, proj]
    if n_past:
        assert k_past.shape[1] % KP == 0
        past = pl.BlockSpec((1, k_past.shape[1], SB_DH), lambda b, h, i: (b, 0, h))
        in_specs += [past, past]
        args += [k_past, v_past]
    return pl.pallas_call(
        functools.partial(_sb_kernel, QB=QB, KP=KP, n_past=n_past),
        grid=(bsz, SB_HEADS, t // QB),
        in_specs=in_specs,
        out_specs=pl.BlockSpec((1, QB, SB_DH), lambda b, h, i: (b, i, h)),
        out_shape=jax.ShapeDtypeStruct((bsz, t, SB_W), BF16),
        scratch_shapes=[pltpu.VMEM((t, SB_DH), BF16), pltpu.VMEM((t, SB_DH), BF16),
                        pltpu.VMEM((QB, SB_DH), F32), pltpu.VMEM((QB, LANE), F32)],
        compiler_params=pltpu.CompilerParams(
            dimension_semantics=("parallel", "parallel", "arbitrary"), vmem_limit_bytes=VMEM_LIMIT),
        name="stick_breaking",
    )(*args)


def _outproj_kernel(*refs, final):
    if final:
        x_ref, a_ref, b_ref, c_ref, w_ref, fw_ref, o_ref = refs
    else:
        x_ref, a_ref, b_ref, c_ref, w_ref, o_ref = refs
    y = jnp.dot(a_ref[...], w_ref[0:HG_W, :], preferred_element_type=F32)
    y = y + jnp.dot(b_ref[...], w_ref[HG_W:HG_W + SSM_INNER, :], preferred_element_type=F32)
    y = y + jnp.dot(c_ref[...], w_ref[HG_W + SSM_INNER:, :], preferred_element_type=F32)
    x = x_ref[...] + y
    if final:
        ms = jnp.mean(x * x, axis=-1, keepdims=True)
        x = x * lax.rsqrt(ms + EPS) * fw_ref[...]
    o_ref[...] = x


def _out_proj(x2d, o_hg, y_ssm, o_sb, w_bf, final_w):
    m, d = x2d.shape
    tm = min(1024, m)
    final = final_w is not None

    def rows(w):
        return pl.BlockSpec((tm, w), lambda i: (i, 0))

    in_specs = [rows(d), rows(HG_W), rows(SSM_INNER), rows(SB_W),
                pl.BlockSpec(w_bf.shape, lambda i: (0, 0))]
    args = [x2d, o_hg, y_ssm, o_sb, w_bf]
    if final:
        in_specs.append(pl.BlockSpec((1, d), lambda i: (0, 0)))
        args.append(final_w.reshape(1, d))
    return pl.pallas_call(
        functools.partial(_outproj_kernel, final=final),
        grid=(m // tm,),
        in_specs=in_specs,
        out_specs=rows(d),
        out_shape=jax.ShapeDtypeStruct((m, d), F32),
        compiler_params=pltpu.CompilerParams(
            dimension_semantics=("parallel",), vmem_limit_bytes=VMEM_LIMIT),
        name="out_proj",
    )(*args)


def _reorder_w_in(w_in):
    hg_cols = 4 * HG_W
    ssm_cols = SSM_INNER + CONV_DIM + SSM_HEADS
    hg = w_in[..., :hg_cols]
    z_xbc = w_in[..., hg_cols:hg_cols + SSM_INNER + CONV_DIM]
    dt = w_in[..., hg_cols + SSM_INNER + CONV_DIM:hg_cols + ssm_cols]
    sb = w_in[..., hg_cols + ssm_cols:]
    used = hg.shape[-1] + z_xbc.shape[-1] + sb.shape[-1] + dt.shape[-1]
    pad = jnp.zeros(w_in.shape[:-1] + (NP - used,), w_in.dtype)
    return jnp.concatenate([hg, z_xbc, sb, dt, pad], axis=-1).astype(BF16)


def _lower_bounds(lb_logits):
    p = jax.nn.softmax(lb_logits.astype(F32), axis=0)
    return jnp.clip(jnp.cumsum(p, axis=0) - p[0:1], 0.0, 1.0)


def _layer(x, states, lb, norm_w, w_in_bf, conv_w, conv_b, dt_bias, a_log, d_skip, ssm_norm_w, hg_norm_w,
           w_out_bf, final_w):
    bsz, t, d = x.shape
    conv0, ssm0, hg0, k_past, v_past = states
    x2d = x.reshape(bsz * t, d)
    proj = _in_proj(x2d, norm_w, w_in_bf).reshape(bsz, t, NP)
    o_hg, hg_state = _hgrn(proj, lb, hg_norm_w, hg0)
    y_ssm, ssm_state, conv_state = _ssd(proj, conv_w, conv_b, dt_bias, a_log, d_skip, ssm_norm_w, conv0, ssm0)
    if k_past is not None:
        k_past = k_past.reshape(bsz, k_past.shape[1], SB_W)
        v_past = v_past.reshape(bsz, v_past.shape[1], SB_W)
    o_sb = _stick_breaking(proj, k_past, v_past)
    x_new = _out_proj(x2d, o_hg.reshape(bsz * t, HG_W), y_ssm.reshape(bsz * t, SSM_INNER),
                      o_sb.reshape(bsz * t, SB_W), w_out_bf, final_w).reshape(bsz, t, d)
    sk = proj[:, :, NB_SB_K * LANE:NB_SB_V * LANE].reshape(bsz, t, SB_HEADS, SB_DH)
    sv = proj[:, :, NB_SB_V * LANE:NB_SB_G * LANE].reshape(bsz, t, SB_HEADS, SB_DH)
    return x_new, (hg_state, ssm_state, conv_state, sk, sv)


def _run_group(x, layer_states, lbs, w_in_bf, w_out_bf, params, final_norm_w):
    norm_w, conv_w, conv_b, dt_bias, a_log, d_skip, ssm_norm_w, hg_norm_w = params
    depth = norm_w.shape[0]
    outs = []
    for l in range(depth):
        x, st = _layer(x, layer_states(l), lbs[l], norm_w[l], w_in_bf[l], conv_w[l], conv_b[l], dt_bias[l],
                       a_log[l], d_skip[l], ssm_norm_w[l], hg_norm_w[l], w_out_bf[l],
                       final_norm_w if l == depth - 1 else None)
        outs.append(st)
    return x, tuple(jnp.stack([st[i] for st in outs]) for i in range(5))


def kernel(x_prompt, x_sample, state_hgrn, state_ssm, state_conv, cache_k, cache_v, norm_w, w_in, lb_logits,
           conv_w, conv_b, dt_bias, a_log, d_skip, ssm_norm_w, hg_norm_w, w_out, final_norm_w):
    lbs = _lower_bounds(lb_logits)
    w_in_bf = _reorder_w_in(w_in)
    w_out_bf = w_out.astype(BF16)
    params = (norm_w, conv_w, conv_b, dt_bias, a_log, d_skip, ssm_norm_w, hg_norm_w)

    y_prompt, p_st = _run_group(x_prompt, lambda l: (None, None, None, None, None),
                                lbs, w_in_bf, w_out_bf, params, final_norm_w)
    y_sample, s_st = _run_group(x_sample,
                                lambda l: (state_conv[l], state_ssm[l], state_hgrn[l], cache_k[l], cache_v[l]),
                                lbs, w_in_bf, w_out_bf, params, final_norm_w)
    return (y_prompt, y_sample) + p_st + s_st
```

```python
import functools

import jax
import jax.numpy as jnp
from jax import lax
from jax.experimental import pallas as pl
from jax.experimental.pallas import tpu as pltpu

F32 = jnp.float32
BF16 = jnp.bfloat16
EPS = 1e-6

LANE = 128
HG_HEADS = 4
HG_D = 128
SSM_HEADS = 16
SSM_P = 64
SSM_N = 128
SSM_GROUPS = 2
SSM_INNER = SSM_HEADS * SSM_P
CONV_W = 4
CONV_DIM = SSM_INNER + 2 * SSM_GROUPS * SSM_N
SB_HEADS = 4
SB_DH = 128
HG_W = HG_HEADS * HG_D
SB_W = SB_HEADS * SB_DH

PROJ_TN = 4 * LANE
QKV_TILES = 3
NB_HG_Q, NB_HG_F, NB_HG_I, NB_HG_G = 0, 4, 8, 12
NB_Z = 16
NB_XBC = 24
NB_SB_G = 36
NB_DT = 40
REST_BLOCKS = 44
NR = REST_BLOCKS * LANE
NP = QKV_TILES * PROJ_TN + NR
LOG2E = 1.4426950408889634

VMEM_LIMIT = 56 * 1024 * 1024

SSD_PAD = 8


def _sigmoid(x):
    return 1.0 / (1.0 + jnp.exp(-x))


def _log_sigmoid(x):
    return jnp.minimum(x, 0.0) - jnp.log1p(jnp.exp(-jnp.abs(x)))


def _cumsum_rows(x):
    n = x.shape[0]
    row = lax.broadcasted_iota(jnp.int32, x.shape, 0)
    sh = 1
    while sh < n:
        x = x + jnp.where(row >= sh, pltpu.roll(x, sh, axis=0), 0.0)
        sh *= 2
    return x


def _transpose_rows(x):
    n = x.shape[0]
    if n < LANE:
        x = jnp.concatenate([x, jnp.zeros((LANE - n, LANE), x.dtype)], axis=0)
    return x.T[:, :n]


def _inproj_kernel(*refs, aliased):
    if aliased:
        x_ref, nw_ref, w_ref, _, _, rest_ref, qkv_ref, k_ref, v_ref, xn_ref = refs
    else:
        x_ref, nw_ref, w_ref, rest_ref, qkv_ref, k_ref, v_ref, xn_ref = refs
    j = pl.program_id(1)

    @pl.when(j == 0)
    def _():
        x = x_ref[...]
        ms = jnp.mean(x * x, axis=-1, keepdims=True)
        xn_ref[...] = (x * lax.rsqrt(ms + EPS) * nw_ref[...]).astype(BF16)

    acc = jnp.dot(xn_ref[...], w_ref[...], preferred_element_type=F32)

    @pl.when(j < QKV_TILES)
    def _():
        qkv_ref[...] = acc.astype(BF16)

    @pl.when(j == 1)
    def _():
        k_ref[0] = acc

    @pl.when(j == 2)
    def _():
        v_ref[0] = acc

    @pl.when(j >= QKV_TILES)
    def _():
        rest_ref[...] = acc


def _in_proj(x2d, norm_w, w_bf, layer, depth, k_all, v_all):
    m, d = x2d.shape
    tm = min(1024, m)
    aliased = k_all is not None
    in_specs = [
        pl.BlockSpec((tm, d), lambda i, j: (i, 0)),
        pl.BlockSpec((1, d), lambda i, j: (0, 0)),
        pl.BlockSpec((d, PROJ_TN), lambda i, j: (0, j)),
    ]
    args = [x2d, norm_w.reshape(1, d), w_bf]
    if aliased:
        in_specs += [pl.BlockSpec(memory_space=pl.ANY), pl.BlockSpec(memory_space=pl.ANY)]
        args += [k_all, v_all]
    kv_spec = pl.BlockSpec((1, tm, SB_W), lambda i, j: (layer, i, 0))
    kv_shape = jax.ShapeDtypeStruct((depth, m, SB_W), F32)
    return pl.pallas_call(
        functools.partial(_inproj_kernel, aliased=aliased),
        grid=(m // tm, NP // PROJ_TN),
        in_specs=in_specs,
        out_specs=[pl.BlockSpec((tm, PROJ_TN), lambda i, j: (i, jnp.maximum(j - QKV_TILES, 0))),
                   pl.BlockSpec((tm, PROJ_TN), lambda i, j: (i, jnp.minimum(j, QKV_TILES - 1))),
                   kv_spec, kv_spec],
        out_shape=[jax.ShapeDtypeStruct((m, NR), F32),
                   jax.ShapeDtypeStruct((m, QKV_TILES * PROJ_TN), BF16),
                   kv_shape, kv_shape],
        input_output_aliases={3: 2, 4: 3} if aliased else {},
        scratch_shapes=[pltpu.VMEM((tm, d), BF16)],
        compiler_params=pltpu.CompilerParams(
            dimension_semantics=("parallel", "arbitrary"), vmem_limit_bytes=VMEM_LIMIT),
        name="in_proj",
    )(*args)


def _hgrn_kernel(*refs, C, has_state):
    if has_state:
        q_ref, f_ref, i_ref, g_ref, lb_ref, nw_ref, s0_ref, o_ref, st_ref, sT_ref, b_ref = refs
    else:
        q_ref, f_ref, i_ref, g_ref, lb_ref, nw_ref, o_ref, st_ref, sT_ref, b_ref = refs
    c = pl.program_id(2)

    @pl.when(c == 0)
    def _():
        if has_state:
            sT_ref[...] = s0_ref[0, 0].T
        else:
            sT_ref[...] = jnp.zeros((HG_D, HG_D), F32)

    hq = q_ref[0]
    hf = f_ref[0]
    v = i_ref[0]
    gate = g_ref[0]
    lb = lb_ref[...]

    q = hq * _sigmoid(hq)
    e = jnp.exp(-jnp.abs(hf))
    la = jnp.log(lb)
    lc = jnp.log1p(-lb) + (jnp.minimum(hf, 0.0) - jnp.log1p(e))
    log_f = jnp.maximum(la, lc) + jnp.log1p(jnp.exp(-jnp.abs(la - lc)))
    k = (1.0 - lb) * (jnp.where(hf >= 0.0, e, 1.0) / (1.0 + e))

    b = _cumsum_rows(log_f)
    b_ref[...] = b
    row = lax.broadcasted_iota(jnp.int32, (C, HG_D), 0)

    att = jnp.zeros((C, C), F32)
    n = C
    while n >= 16:
        h = n // 2
        pieces = [jnp.broadcast_to(b_ref[blk * n + h - 1:blk * n + h, :], (n, HG_D)) for blk in range(C // n)]
        b_mid = pieces[0] if len(pieces) == 1 else jnp.concatenate(pieces, axis=0)
        second = (row & (n - 1)) >= h
        ex = jnp.exp(jnp.where(second, b - b_mid, b_mid - b))
        q_n = jnp.where(second, q * ex, 0.0).astype(BF16)
        k_n = jnp.where(second, 0.0, k * ex).astype(BF16)
        a_n = lax.dot_general(q_n, k_n, (((1,), (1,)), ((), ())), preferred_element_type=F32)
        if n < C:
            shift = n.bit_length() - 1
            rb = lax.broadcasted_iota(jnp.int32, (C, C), 0) >> shift
            cb = lax.broadcasted_iota(jnp.int32, (C, C), 1) >> shift
            a_n = jnp.where(rb == cb, a_n, 0.0)
        att = att + a_n
        n = h

    ones = jnp.ones((HG_D, HG_D), BF16)
    r8 = row & 7
    o = jnp.zeros((C, HG_D), F32)
    for d in range(8):
        if d == 0:
            p = q * k
            vs = v
        else:
            ks = pltpu.roll(k, d, axis=0)
            bs = pltpu.roll(b, d, axis=0)
            vs = pltpu.roll(v, d, axis=0)
            p = jnp.where(r8 >= d, q * ks * jnp.exp(b - bs), 0.0)
        a_d = jnp.dot(p.astype(BF16), ones, preferred_element_type=F32)
        o = o + a_d * vs

    s_t = sT_ref[...]
    v_bf = v.astype(BF16)
    o = o + lax.dot_general((q * jnp.exp(b)).astype(BF16), s_t.astype(BF16),
                            (((1,), (1,)), ((), ())), preferred_element_type=F32)
    o = o + jnp.dot(att.astype(BF16), v_bf, preferred_element_type=F32)

    b_last = b_ref[C - 1:C, :]
    k_dec = (k * jnp.exp(b_last - b)).astype(BF16)
    upd = lax.dot_general(v_bf, k_dec, (((0,), (0,)), ((), ())), preferred_element_type=F32)
    sT_ref[...] = s_t * jnp.exp(b_last) + upd

    ms = jnp.mean(o * o, axis=-1, keepdims=True)
    o = o * lax.rsqrt(ms + EPS) * nw_ref[...]
    o_ref[0] = (o * (gate * _sigmoid(gate))).astype(o_ref.dtype)

    @pl.when(c == pl.num_programs(2) - 1)
    def _():
        st_ref[0, 0] = sT_ref[...].T


def _hgrn(proj, lb, nw, s0):
    bsz, t, _ = proj.shape
    C = min(128, t)
    has_state = s0 is not None

    def col(nb):
        return pl.BlockSpec((1, C, HG_D), lambda b, h, c: (b, c, nb + h))

    in_specs = [col(NB_HG_Q), col(NB_HG_F), col(NB_HG_I), col(NB_HG_G),
                pl.BlockSpec((1, HG_D), lambda b, h, c: (0, h)),
                pl.BlockSpec((1, HG_D), lambda b, h, c: (0, 0))]
    args = [proj, proj, proj, proj, lb.reshape(1, HG_W), nw.reshape(1, HG_D)]
    if has_state:
        in_specs.append(pl.BlockSpec((1, 1, HG_D, HG_D), lambda b, h, c: (b, h, 0, 0)))
        args.append(s0)
    return pl.pallas_call(
        functools.partial(_hgrn_kernel, C=C, has_state=has_state),
        grid=(bsz, HG_HEADS, t // C),
        in_specs=in_specs,
        out_specs=[pl.BlockSpec((1, C, HG_D), lambda b, h, c: (b, c, h)),
                   pl.BlockSpec((1, 1, HG_D, HG_D), lambda b, h, c: (b, h, 0, 0))],
        out_shape=[jax.ShapeDtypeStruct((bsz, t, HG_W), BF16),
                   jax.ShapeDtypeStruct((bsz, HG_HEADS, HG_D, HG_D), F32)],
        scratch_shapes=[pltpu.VMEM((HG_D, HG_D), F32), pltpu.VMEM((C, HG_D), F32)],
        compiler_params=pltpu.CompilerParams(
            dimension_semantics=("parallel", "parallel", "arbitrary"), vmem_limit_bytes=VMEM_LIMIT),
        name="hgrn2_scan",
    )(*args)


def _ssd_kernel(*refs, L, has_state):
    if has_state:
        (z_ref, xbc_ref, dt_ref, cw_ref, cb_ref, dtb_ref, al_ref, dsk_ref, nw_ref, conv0_ref, s0_ref,
         y_ref, st_ref, cst_ref, xp_ref, s_ref, ys_ref) = refs
    else:
        (z_ref, xbc_ref, dt_ref, cw_ref, cb_ref, dtb_ref, al_ref, dsk_ref, nw_ref,
         y_ref, st_ref, cst_ref, xp_ref, s_ref, ys_ref) = refs
    c = pl.program_id(1)

    @pl.when(c == 0)
    def _():
        xp_ref[0:SSD_PAD, :] = jnp.zeros((SSD_PAD, CONV_DIM), F32)
        if has_state:
            xp_ref[SSD_PAD - (CONV_W - 1):SSD_PAD, :] = conv0_ref[0]
            s_ref[...] = s0_ref[0]
        else:
            s_ref[...] = jnp.zeros((SSM_HEADS, SSM_P, SSM_N), F32)

    xp_ref[SSD_PAD:SSD_PAD + L, :] = xbc_ref[0]
    cw = cw_ref[...]
    conv = cb_ref[...]
    for j in range(CONV_W):
        lo = SSD_PAD - (CONV_W - 1) + j
        conv = conv + cw[j:j + 1, :] * xp_ref[lo:lo + L, :]
    cst_ref[0] = xp_ref[L + SSD_PAD - (CONV_W - 1):L + SSD_PAD, :]
    xp_ref[0:SSD_PAD, :] = xp_ref[L:L + SSD_PAD, :]
    act = conv * _sigmoid(conv)
    xs = act[:, :SSM_INNER]

    dt_raw = dt_ref[0] + dtb_ref[...]
    dt = jnp.maximum(dt_raw, 0.0) + jnp.log1p(jnp.exp(-jnp.abs(dt_raw)))
    a = -jnp.exp(al_ref[...])
    cs = _cumsum_rows(dt * a)
    cs_t = _transpose_rows(cs)
    dt_t = _transpose_rows(dt)

    tril = lax.broadcasted_iota(jnp.int32, (L, L), 0) >= lax.broadcasted_iota(jnp.int32, (L, L), 1)
    hpg = SSM_HEADS // SSM_GROUPS
    for g in range(SSM_GROUPS):
        b_g = act[:, SSM_INNER + g * SSM_N:SSM_INNER + (g + 1) * SSM_N].astype(BF16)
        c_g = act[:, SSM_INNER + (SSM_GROUPS + g) * SSM_N:SSM_INNER + (SSM_GROUPS + g + 1) * SSM_N].astype(BF16)
        cb = lax.dot_general(c_g, b_g, (((1,), (1,)), ((), ())), preferred_element_type=F32)
        for hh in range(hpg):
            h = g * hpg + hh
            cs_col = cs[:, h:h + 1]
            cs_row = cs_t[h:h + 1, :]
            decay = jnp.exp(jnp.where(tril, cs_col - cs_row, -jnp.inf))
            w = (cb * decay * dt_t[h:h + 1, :]).astype(BF16)
            x_h = xs[:, h * SSM_P:(h + 1) * SSM_P]
            s_h = s_ref[h]
            y_h = jnp.dot(w, x_h.astype(BF16), preferred_element_type=F32)
            y_h = y_h + jnp.exp(cs_col) * lax.dot_general(
                c_g, s_h.astype(BF16), (((1,), (1,)), ((), ())), preferred_element_type=F32)
            ys_ref[:, h * SSM_P:(h + 1) * SSM_P] = y_h
            cs_last = cs_t[h:h + 1, L - 1:L]
            wdec = jnp.exp(cs_last - cs_col) * dt[:, h:h + 1]
            upd = lax.dot_general((x_h * wdec).astype(BF16), b_g, (((0,), (0,)), ((), ())),
                                  preferred_element_type=F32)
            s_ref[h] = jnp.exp(cs_last) * s_h + upd

    z = z_ref[0]
    y = (ys_ref[...] + dsk_ref[...] * xs) * (z * _sigmoid(z))
    gsz = SSM_INNER // SSM_GROUPS
    nw = nw_ref[...]
    for g in range(SSM_GROUPS):
        yg = y[:, g * gsz:(g + 1) * gsz]
        ms = jnp.mean(yg * yg, axis=-1, keepdims=True)
        y_ref[0, :, g * gsz:(g + 1) * gsz] = (yg * lax.rsqrt(ms + EPS) * nw[:, g * gsz:(g + 1) * gsz]).astype(y_ref.dtype)

    @pl.when(c == pl.num_programs(1) - 1)
    def _():
        st_ref[0] = s_ref[...]


def _pad_lanes(v):
    return jnp.pad(v.astype(F32), (0, LANE - v.shape[0])).reshape(1, LANE)


def _ssd(proj, conv_w, conv_b, dt_bias, a_log, d_skip, ssm_norm_w, conv0, s0):
    bsz, t, _ = proj.shape
    L = min(256, t)
    has_state = s0 is not None

    def const(shape):
        return pl.BlockSpec(shape, lambda b, c: (0,) * len(shape))

    in_specs = [pl.BlockSpec((1, L, SSM_INNER), lambda b, c: (b, c, NB_Z * LANE // SSM_INNER)),
                pl.BlockSpec((1, L, CONV_DIM), lambda b, c: (b, c, NB_XBC * LANE // CONV_DIM)),
                pl.BlockSpec((1, L, LANE), lambda b, c: (b, c, NB_DT)),
                const((CONV_W, CONV_DIM)), const((1, CONV_DIM)), const((1, LANE)), const((1, LANE)),
                const((1, SSM_INNER)), const((1, SSM_INNER))]
    args = [proj, proj, proj, conv_w, conv_b.reshape(1, CONV_DIM), _pad_lanes(dt_bias), _pad_lanes(a_log),
            jnp.repeat(d_skip.astype(F32), SSM_P).reshape(1, SSM_INNER), ssm_norm_w.reshape(1, SSM_INNER)]
    if has_state:
        in_specs += [pl.BlockSpec((1, CONV_W - 1, CONV_DIM), lambda b, c: (b, 0, 0)),
                     pl.BlockSpec((1, SSM_HEADS, SSM_P, SSM_N), lambda b, c: (b, 0, 0, 0))]
        args += [conv0, s0]
    return pl.pallas_call(
        functools.partial(_ssd_kernel, L=L, has_state=has_state),
        grid=(bsz, t // L),
        in_specs=in_specs,
        out_specs=[pl.BlockSpec((1, L, SSM_INNER), lambda b, c: (b, c, 0)),
                   pl.BlockSpec((1, SSM_HEADS, SSM_P, SSM_N), lambda b, c: (b, 0, 0, 0)),
                   pl.BlockSpec((1, CONV_W - 1, CONV_DIM), lambda b, c: (b, 0, 0))],
        out_shape=[jax.ShapeDtypeStruct((bsz, t, SSM_INNER), BF16),
                   jax.ShapeDtypeStruct((bsz, SSM_HEADS, SSM_P, SSM_N), F32),
                   jax.ShapeDtypeStruct((bsz, CONV_W - 1, CONV_DIM), F32)],
        scratch_shapes=[pltpu.VMEM((L + SSD_PAD, CONV_DIM), F32),
                        pltpu.VMEM((SSM_HEADS, SSM_P, SSM_N), F32),
                        pltpu.VMEM((L, SSM_INNER), F32)],
        compiler_params=pltpu.CompilerParams(
            dimension_semantics=("parallel", "arbitrary"), vmem_limit_bytes=VMEM_LIMIT),
        name="conv_ssd_scan",
    )(*args)


def _neg_tri_ext(kb):
    kk = lax.broadcasted_iota(jnp.int32, (kb, kb + LANE), 0)
    jj = lax.broadcasted_iota(jnp.int32, (kb, kb + LANE), 1)
    return jnp.where((kk >= jj) | (jj >= kb), -1.0, 0.0).astype(BF16)


def _sb_kernel(*refs, QB, KP, n_past):
    if n_past:
        q_ref, k_ref, v_ref, g_ref, kp_ref, vp_ref, o_ref = refs[:7]
        scr = refs[7:]
    else:
        q_ref, k_ref, v_ref, g_ref, o_ref = refs[:5]
        scr = refs[5:]
    nh = SB_HEADS
    acc_refs, r_refs, xn_ref, tn_ref = scr[0:nh], scr[nh:2 * nh], scr[2 * nh:3 * nh], scr[3 * nh:4 * nh]
    if n_past:
        xp_ref, tp_ref = scr[4 * nh:5 * nh], scr[5 * nh:6 * nh]
    qi = pl.program_id(1)
    nt = (((1,), (1,)), ((), ()))

    def lanes(h):
        return slice(h * SB_DH, (h + 1) * SB_DH)

    def scores(h, k_t, ntri, x_ref, t_ref, mask=None):
        kb = k_t.shape[0]
        z = lax.dot_general(q_ref[0, :, lanes(h)], k_t, nt, preferred_element_type=F32)
        sp = jnp.maximum(z, 0.0) + jnp.log2(1.0 + jnp.exp2(-jnp.abs(z)))
        if mask is not None:
            sp = jnp.where(mask, sp, 0.0)
        wt = jnp.dot(sp.astype(BF16), ntri, preferred_element_type=F32)
        x = z + wt[:, :kb]
        if mask is not None:
            x = jnp.where(mask, x, -jnp.inf)
        x_ref[h][...] = x
        t_ref[h][...] = wt[:, kb:]

    def accumulate(h, v_t, x_ref, t_ref):
        kb = v_t.shape[0]
        r = r_refs[h][...]
        w = jnp.exp2(x_ref[h][...] + jnp.tile(r, (1, max(kb // LANE, 1)))[:, :kb])
        acc_refs[h][...] += jnp.dot(w.astype(BF16), v_t, preferred_element_type=F32)
        r_refs[h][...] = r + t_ref[h][...]

    for h in range(SB_HEADS):
        acc_refs[h][...] = jnp.zeros_like(acc_refs[h])
        r_refs[h][...] = jnp.zeros_like(r_refs[h])

    start = pl.multiple_of(qi * QB, QB)
    mask = lax.broadcasted_iota(jnp.int32, (QB, QB), 1) < lax.broadcasted_iota(jnp.int32, (QB, QB), 0)
    ntri_q = _neg_tri_ext(QB)
    for h in range(SB_HEADS):
        scores(h, k_ref[0, pl.ds(start, QB), lanes(h)], ntri_q, xn_ref, tn_ref, mask=mask)

    if QB % LANE == 0:
        def new_body(jj, carry):
            s = pl.multiple_of((qi - 1 - jj) * QB, QB)
            for h in range(SB_HEADS):
                accumulate(h, v_ref[0, pl.ds(s + QB, QB), lanes(h)], xn_ref, tn_ref)
                scores(h, k_ref[0, pl.ds(s, QB), lanes(h)], ntri_q, xn_ref, tn_ref)
            return carry

        lax.fori_loop(0, qi, new_body, 0)

    if n_past:
        ntri_p = _neg_tri_ext(KP)
        last = (n_past - 1) * KP
        for h in range(SB_HEADS):
            accumulate(h, v_ref[0, 0:QB, lanes(h)], xn_ref, tn_ref)
            scores(h, kp_ref[0, last:last + KP, lanes(h)].astype(BF16), ntri_p, xp_ref, tp_ref)

        def past_body(jj, carry):
            s = pl.multiple_of((n_past - 1 - jj) * KP, KP)
            for h in range(SB_HEADS):
                accumulate(h, vp_ref[0, pl.ds(s + KP, KP), lanes(h)].astype(BF16), xp_ref, tp_ref)
                scores(h, kp_ref[0, pl.ds(s, KP), lanes(h)].astype(BF16), ntri_p, xp_ref, tp_ref)
            return carry

        lax.fori_loop(1, n_past, past_body, 0)
        for h in range(SB_HEADS):
            accumulate(h, vp_ref[0, 0:KP, lanes(h)].astype(BF16), xp_ref, tp_ref)
    else:
        for h in range(SB_HEADS):
            accumulate(h, v_ref[0, 0:QB, lanes(h)], xn_ref, tn_ref)

    gate = g_ref[0]
    for h in range(SB_HEADS):
        gh = gate[:, lanes(h)]
        o_ref[0, :, lanes(h)] = (acc_refs[h][...] * (gh * _sigmoid(gh))).astype(o_ref.dtype)


def _stick_breaking(rest, qkv, k_past, v_past):
    bsz, t, _ = rest.shape
    QB = min(256, t)
    KP = 256
    n_past = 0 if k_past is None else k_past.shape[1] // KP
    assert t % QB == 0 and (t == QB or QB % LANE == 0)

    in_specs = [pl.BlockSpec((1, QB, SB_W), lambda b, i: (b, i, 0)),
                pl.BlockSpec((1, t, SB_W), lambda b, i: (b, 0, 1)),
                pl.BlockSpec((1, t, SB_W), lambda b, i: (b, 0, 2)),
                pl.BlockSpec((1, QB, SB_W), lambda b, i: (b, i, NB_SB_G * LANE // SB_W))]
    args = [qkv, qkv, qkv, rest]
    if n_past:
        assert k_past.shape[1] % KP == 0
        past = pl.BlockSpec((1, k_past.shape[1], SB_W), lambda b, i: (b, 0, 0))
        in_specs += [past, past]
        args += [k_past, v_past]
    return pl.pallas_call(
        functools.partial(_sb_kernel, QB=QB, KP=KP, n_past=n_past),
        grid=(bsz, t // QB),
        in_specs=in_specs,
        out_specs=pl.BlockSpec((1, QB, SB_W), lambda b, i: (b, i, 0)),
        out_shape=jax.ShapeDtypeStruct((bsz, t, SB_W), BF16),
        scratch_shapes=[pltpu.VMEM((QB, SB_DH), F32)] * (2 * SB_HEADS) + [pltpu.VMEM((QB, QB), F32)] * SB_HEADS
        + [pltpu.VMEM((QB, LANE), F32)] * SB_HEADS
        + ([pltpu.VMEM((QB, KP), F32)] * SB_HEADS + [pltpu.VMEM((QB, LANE), F32)] * SB_HEADS if n_past else []),
        compiler_params=pltpu.CompilerParams(
            dimension_semantics=("parallel", "arbitrary"), vmem_limit_bytes=VMEM_LIMIT),
        name="stick_breaking",
    )(*args)


def _outproj_kernel(*refs, final):
    if final:
        x_ref, a_ref, b_ref, c_ref, w_ref, fw_ref, o_ref = refs
    else:
        x_ref, a_ref, b_ref, c_ref, w_ref, o_ref = refs
    y = jnp.dot(a_ref[...], w_ref[0:HG_W, :], preferred_element_type=F32)
    y = y + jnp.dot(b_ref[...], w_ref[HG_W:HG_W + SSM_INNER, :], preferred_element_type=F32)
    y = y + jnp.dot(c_ref[...], w_ref[HG_W + SSM_INNER:, :], preferred_element_type=F32)
    x = x_ref[...] + y
    if final:
        ms = jnp.mean(x * x, axis=-1, keepdims=True)
        x = x * lax.rsqrt(ms + EPS) * fw_ref[...]
    o_ref[...] = x


def _out_proj(x2d, o_hg, y_ssm, o_sb, w_bf, final_w):
    m, d = x2d.shape
    tm = min(1024, m)
    final = final_w is not None

    def rows(w):
        return pl.BlockSpec((tm, w), lambda i: (i, 0))

    in_specs = [rows(d), rows(HG_W), rows(SSM_INNER), rows(SB_W),
                pl.BlockSpec(w_bf.shape, lambda i: (0, 0))]
    args = [x2d, o_hg, y_ssm, o_sb, w_bf]
    if final:
        in_specs.append(pl.BlockSpec((1, d), lambda i: (0, 0)))
        args.append(final_w.reshape(1, d))
    return pl.pallas_call(
        functools.partial(_outproj_kernel, final=final),
        grid=(m // tm,),
        in_specs=in_specs,
        out_specs=rows(d),
        out_shape=jax.ShapeDtypeStruct((m, d), F32),
        compiler_params=pltpu.CompilerParams(
            dimension_semantics=("parallel",), vmem_limit_bytes=VMEM_LIMIT),
        name="out_proj",
    )(*args)


def _reorder_w_in(w_in):
    hg_cols = 4 * HG_W
    ssm_cols = SSM_INNER + CONV_DIM + SSM_HEADS
    hg = w_in[..., :hg_cols]
    z_xbc = w_in[..., hg_cols:hg_cols + SSM_INNER + CONV_DIM]
    dt = w_in[..., hg_cols + SSM_INNER + CONV_DIM:hg_cols + ssm_cols]
    sb = w_in[..., hg_cols + ssm_cols:]
    sb_q = sb[..., :SB_W] * (LOG2E * SB_DH ** -0.5)
    sb_kv = sb[..., SB_W:3 * SB_W]
    sb_g = sb[..., 3 * SB_W:]
    used = w_in.shape[-1]
    pad = jnp.zeros(w_in.shape[:-1] + (NP - used,), w_in.dtype)
    return jnp.concatenate([sb_q, sb_kv, hg, z_xbc, sb_g, dt, pad], axis=-1).astype(BF16)


def _lower_bounds(lb_logits):
    p = jax.nn.softmax(lb_logits.astype(F32), axis=0)
    return jnp.clip(jnp.cumsum(p, axis=0) - p[0:1], 0.0, 1.0)


def _layer(x, states, kv_all, layer, depth, lb, norm_w, w_in_bf, conv_w, conv_b, dt_bias, a_log, d_skip,
           ssm_norm_w, hg_norm_w, w_out_bf, final_w):
    bsz, t, d = x.shape
    conv0, ssm0, hg0, k_past, v_past = states
    x2d = x.reshape(bsz * t, d)
    rest, qkv, k_all, v_all = _in_proj(x2d, norm_w, w_in_bf, layer, depth, *kv_all)
    rest = rest.reshape(bsz, t, NR)
    qkv = qkv.reshape(bsz, t, QKV_TILES * PROJ_TN)
    o_hg, hg_state = _hgrn(rest, lb, hg_norm_w, hg0)
    y_ssm, ssm_state, conv_state = _ssd(rest, conv_w, conv_b, dt_bias, a_log, d_skip, ssm_norm_w, conv0, ssm0)
    if k_past is not None:
        k_past = k_past.reshape(bsz, k_past.shape[1], SB_W)
        v_past = v_past.reshape(bsz, v_past.shape[1], SB_W)
    o_sb = _stick_breaking(rest, qkv, k_past, v_past)
    x_new = _out_proj(x2d, o_hg.reshape(bsz * t, HG_W), y_ssm.reshape(bsz * t, SSM_INNER),
                      o_sb.reshape(bsz * t, SB_W), w_out_bf, final_w).reshape(bsz, t, d)
    return x_new, (hg_state, ssm_state, conv_state), (k_all, v_all)


def _run_group(x, layer_states, lbs, w_in_bf, w_out_bf, params, final_norm_w):
    norm_w, conv_w, conv_b, dt_bias, a_log, d_skip, ssm_norm_w, hg_norm_w = params
    depth = norm_w.shape[0]
    bsz, t, _ = x.shape
    outs = []
    kv_all = (None, None)
    for l in range(depth):
        x, st, kv_all = _layer(x, layer_states(l), kv_all, l, depth, lbs[l], norm_w[l], w_in_bf[l], conv_w[l],
                               conv_b[l], dt_bias[l], a_log[l], d_skip[l], ssm_norm_w[l], hg_norm_w[l],
                               w_out_bf[l], final_norm_w if l == depth - 1 else None)
        outs.append(st)
    kv = tuple(a.reshape(depth, bsz, t, SB_HEADS, SB_DH) for a in kv_all)
    return x, tuple(jnp.stack([st[i] for st in outs]) for i in range(3)) + kv


def kernel(x_prompt, x_sample, state_hgrn, state_ssm, state_conv, cache_k, cache_v, norm_w, w_in, lb_logits,
           conv_w, conv_b, dt_bias, a_log, d_skip, ssm_norm_w, hg_norm_w, w_out, final_norm_w):
    lbs = _lower_bounds(lb_logits)
    w_in_bf = _reorder_w_in(w_in)
    w_out_bf = w_out.astype(BF16)
    params = (norm_w, conv_w, conv_b, dt_bias, a_log, d_skip, ssm_norm_w, hg_norm_w)

    y_prompt, p_st = _run_group(x_prompt, lambda l: (None, None, None, None, None),
                                lbs, w_in_bf, w_out_bf, params, final_norm_w)
    y_sample, s_st = _run_group(x_sample,
                                lambda l: (state_conv[l], state_ssm[l], state_hgrn[l], cache_k[l], cache_v[l]),
                                lbs, w_in_bf, w_out_bf, params, final_norm_w)
    return (y_prompt, y_sample) + p_st + s_st
```

```python
import functools

import jax
import jax.numpy as jnp
from jax import lax
from jax.experimental import pallas as pl
from jax.experimental.pallas import tpu as pltpu

F32 = jnp.float32
BF16 = jnp.bfloat16
EPS = 1e-6

LANE = 128
HG_HEADS = 4
HG_D = 128
SSM_HEADS = 16
SSM_P = 64
SSM_N = 128
SSM_GROUPS = 2
SSM_INNER = SSM_HEADS * SSM_P
CONV_W = 4
CONV_DIM = SSM_INNER + 2 * SSM_GROUPS * SSM_N
SB_HEADS = 4
SB_DH = 128
HG_W = HG_HEADS * HG_D
SB_W = SB_HEADS * SB_DH

PROJ_TN = 4 * LANE
QKV_TILES = 3
NB_HG_Q, NB_HG_F, NB_HG_I, NB_HG_G = 0, 4, 8, 12
NB_Z = 16
NB_XBC = 24
NB_SB_G = 36
NB_DT = 40
REST_BLOCKS = 44
NR = REST_BLOCKS * LANE
NP = QKV_TILES * PROJ_TN + NR
LOG2E = 1.4426950408889634

VMEM_LIMIT = 56 * 1024 * 1024

SSD_PAD = 8


def _sigmoid(x):
    return 1.0 / (1.0 + jnp.exp(-x))


def _log_sigmoid(x):
    return jnp.minimum(x, 0.0) - jnp.log1p(jnp.exp(-jnp.abs(x)))


def _cumsum_rows(x):
    n = x.shape[0]
    row = lax.broadcasted_iota(jnp.int32, x.shape, 0)
    sh = 1
    while sh < n:
        x = x + jnp.where(row >= sh, pltpu.roll(x, sh, axis=0), 0.0)
        sh *= 2
    return x


def _transpose_rows(x):
    n = x.shape[0]
    if n < LANE:
        x = jnp.concatenate([x, jnp.zeros((LANE - n, LANE), x.dtype)], axis=0)
    return x.T[:, :n]


def _inproj_kernel(*refs, aliased):
    if aliased:
        x_ref, nw_ref, w_ref, _, _, rest_ref, qkv_ref, k_ref, v_ref, xn_ref = refs
    else:
        x_ref, nw_ref, w_ref, rest_ref, qkv_ref, k_ref, v_ref, xn_ref = refs
    j = pl.program_id(1)

    @pl.when(j == 0)
    def _():
        x = x_ref[...]
        ms = jnp.mean(x * x, axis=-1, keepdims=True)
        xn_ref[...] = (x * lax.rsqrt(ms + EPS) * nw_ref[...]).astype(BF16)

    acc = jnp.dot(xn_ref[...], w_ref[...], preferred_element_type=F32)

    @pl.when(j < QKV_TILES)
    def _():
        qkv_ref[...] = acc.astype(BF16)

    @pl.when(j == 1)
    def _():
        k_ref[0] = acc

    @pl.when(j == 2)
    def _():
        v_ref[0] = acc

    @pl.when(j >= QKV_TILES)
    def _():
        rest_ref[...] = acc


def _in_proj(x2d, norm_w, w_bf, layer, depth, k_all, v_all):
    m, d = x2d.shape
    tm = min(1024, m)
    aliased = k_all is not None
    in_specs = [
        pl.BlockSpec((tm, d), lambda i, j: (i, 0)),
        pl.BlockSpec((1, d), lambda i, j: (0, 0)),
        pl.BlockSpec((d, PROJ_TN), lambda i, j: (0, j)),
    ]
    args = [x2d, norm_w.reshape(1, d), w_bf]
    if aliased:
        in_specs += [pl.BlockSpec(memory_space=pl.ANY), pl.BlockSpec(memory_space=pl.ANY)]
        args += [k_all, v_all]
    kv_spec = pl.BlockSpec((1, tm, SB_W), lambda i, j: (layer, i, 0))
    kv_shape = jax.ShapeDtypeStruct((depth, m, SB_W), F32)
    return pl.pallas_call(
        functools.partial(_inproj_kernel, aliased=aliased),
        grid=(m // tm, NP // PROJ_TN),
        in_specs=in_specs,
        out_specs=[pl.BlockSpec((tm, PROJ_TN), lambda i, j: (i, jnp.maximum(j - QKV_TILES, 0))),
                   pl.BlockSpec((tm, PROJ_TN), lambda i, j: (i, jnp.minimum(j, QKV_TILES - 1))),
                   kv_spec, kv_spec],
        out_shape=[jax.ShapeDtypeStruct((m, NR), F32),
                   jax.ShapeDtypeStruct((m, QKV_TILES * PROJ_TN), BF16),
                   kv_shape, kv_shape],
        input_output_aliases={3: 2, 4: 3} if aliased else {},
        scratch_shapes=[pltpu.VMEM((tm, d), BF16)],
        compiler_params=pltpu.CompilerParams(
            dimension_semantics=("parallel", "arbitrary"), vmem_limit_bytes=VMEM_LIMIT),
        name="in_proj",
    )(*args)


def _hgrn_kernel(*refs, C, has_state):
    if has_state:
        q_ref, f_ref, i_ref, g_ref, lb_ref, nw_ref, s0_ref, o_ref, st_ref, sT_ref, b_ref = refs
    else:
        q_ref, f_ref, i_ref, g_ref, lb_ref, nw_ref, o_ref, st_ref, sT_ref, b_ref = refs
    c = pl.program_id(2)

    @pl.when(c == 0)
    def _():
        if has_state:
            sT_ref[...] = s0_ref[0, 0].T
        else:
            sT_ref[...] = jnp.zeros((HG_D, HG_D), F32)

    hq = q_ref[0]
    hf = f_ref[0]
    v = i_ref[0]
    gate = g_ref[0]
    lb = lb_ref[...]

    q = hq * _sigmoid(hq)
    e = jnp.exp(-jnp.abs(hf))
    la = jnp.log(lb)
    lc = jnp.log1p(-lb) + (jnp.minimum(hf, 0.0) - jnp.log1p(e))
    log_f = jnp.maximum(la, lc) + jnp.log1p(jnp.exp(-jnp.abs(la - lc)))
    k = (1.0 - lb) * (jnp.where(hf >= 0.0, e, 1.0) / (1.0 + e))

    b = _cumsum_rows(log_f)
    b_ref[...] = b
    row = lax.broadcasted_iota(jnp.int32, (C, HG_D), 0)

    att = jnp.zeros((C, C), F32)
    n = C
    while n >= 16:
        h = n // 2
        pieces = [jnp.broadcast_to(b_ref[blk * n + h - 1:blk * n + h, :], (n, HG_D)) for blk in range(C // n)]
        b_mid = pieces[0] if len(pieces) == 1 else jnp.concatenate(pieces, axis=0)
        second = (row & (n - 1)) >= h
        ex = jnp.exp(jnp.where(second, b - b_mid, b_mid - b))
        q_n = jnp.where(second, q * ex, 0.0).astype(BF16)
        k_n = jnp.where(second, 0.0, k * ex).astype(BF16)
        a_n = lax.dot_general(q_n, k_n, (((1,), (1,)), ((), ())), preferred_element_type=F32)
        if n < C:
            shift = n.bit_length() - 1
            rb = lax.broadcasted_iota(jnp.int32, (C, C), 0) >> shift
            cb = lax.broadcasted_iota(jnp.int32, (C, C), 1) >> shift
            a_n = jnp.where(rb == cb, a_n, 0.0)
        att = att + a_n
        n = h

    ones = jnp.ones((HG_D, HG_D), BF16)
    r8 = row & 7
    o = jnp.zeros((C, HG_D), F32)
    for d in range(8):
        if d == 0:
            p = q * k
            vs = v
        else:
            ks = pltpu.roll(k, d, axis=0)
            bs = pltpu.roll(b, d, axis=0)
            vs = pltpu.roll(v, d, axis=0)
            p = jnp.where(r8 >= d, q * ks * jnp.exp(b - bs), 0.0)
        a_d = jnp.dot(p.astype(BF16), ones, preferred_element_type=F32)
        o = o + a_d * vs

    s_t = sT_ref[...]
    v_bf = v.astype(BF16)
    o = o + lax.dot_general((q * jnp.exp(b)).astype(BF16), s_t.astype(BF16),
                            (((1,), (1,)), ((), ())), preferred_element_type=F32)
    o = o + jnp.dot(att.astype(BF16), v_bf, preferred_element_type=F32)

    b_last = b_ref[C - 1:C, :]
    k_dec = (k * jnp.exp(b_last - b)).astype(BF16)
    upd = lax.dot_general(v_bf, k_dec, (((0,), (0,)), ((), ())), preferred_element_type=F32)
    sT_ref[...] = s_t * jnp.exp(b_last) + upd

    ms = jnp.mean(o * o, axis=-1, keepdims=True)
    o = o * lax.rsqrt(ms + EPS) * nw_ref[...]
    o_ref[0] = (o * (gate * _sigmoid(gate))).astype(o_ref.dtype)

    @pl.when(c == pl.num_programs(2) - 1)
    def _():
        st_ref[0, 0] = sT_ref[...].T


def _hgrn(proj, lb, nw, s0):
    bsz, t, _ = proj.shape
    C = min(128, t)
    has_state = s0 is not None

    def col(nb):
        return pl.BlockSpec((1, C, HG_D), lambda b, h, c: (b, c, nb + h))

    in_specs = [col(NB_HG_Q), col(NB_HG_F), col(NB_HG_I), col(NB_HG_G),
                pl.BlockSpec((1, HG_D), lambda b, h, c: (0, h)),
                pl.BlockSpec((1, HG_D), lambda b, h, c: (0, 0))]
    args = [proj, proj, proj, proj, lb.reshape(1, HG_W), nw.reshape(1, HG_D)]
    if has_state:
        in_specs.append(pl.BlockSpec((1, 1, HG_D, HG_D), lambda b, h, c: (b, h, 0, 0)))
        args.append(s0)
    return pl.pallas_call(
        functools.partial(_hgrn_kernel, C=C, has_state=has_state),
        grid=(bsz, HG_HEADS, t // C),
        in_specs=in_specs,
        out_specs=[pl.BlockSpec((1, C, HG_D), lambda b, h, c: (b, c, h)),
                   pl.BlockSpec((1, 1, HG_D, HG_D), lambda b, h, c: (b, h, 0, 0))],
        out_shape=[jax.ShapeDtypeStruct((bsz, t, HG_W), BF16),
                   jax.ShapeDtypeStruct((bsz, HG_HEADS, HG_D, HG_D), F32)],
        scratch_shapes=[pltpu.VMEM((HG_D, HG_D), F32), pltpu.VMEM((C, HG_D), F32)],
        compiler_params=pltpu.CompilerParams(
            dimension_semantics=("parallel", "parallel", "arbitrary"), vmem_limit_bytes=VMEM_LIMIT),
        name="hgrn2_scan",
    )(*args)


def _ssd_kernel(*refs, L, has_state):
    if has_state:
        (z_ref, xbc_ref, dt_ref, cw_ref, cb_ref, dtb_ref, al_ref, dsk_ref, nw_ref, conv0_ref, s0_ref,
         y_ref, st_ref, cst_ref, xp_ref, s_ref, ys_ref) = refs
    else:
        (z_ref, xbc_ref, dt_ref, cw_ref, cb_ref, dtb_ref, al_ref, dsk_ref, nw_ref,
         y_ref, st_ref, cst_ref, xp_ref, s_ref, ys_ref) = refs
    c = pl.program_id(1)

    @pl.when(c == 0)
    def _():
        xp_ref[0:SSD_PAD, :] = jnp.zeros((SSD_PAD, CONV_DIM), F32)
        if has_state:
            xp_ref[SSD_PAD - (CONV_W - 1):SSD_PAD, :] = conv0_ref[0]
            s_ref[...] = s0_ref[0]
        else:
            s_ref[...] = jnp.zeros((SSM_HEADS, SSM_P, SSM_N), F32)

    xp_ref[SSD_PAD:SSD_PAD + L, :] = xbc_ref[0]
    cw = cw_ref[...]
    conv = cb_ref[...]
    for j in range(CONV_W):
        lo = SSD_PAD - (CONV_W - 1) + j
        conv = conv + cw[j:j + 1, :] * xp_ref[lo:lo + L, :]
    cst_ref[0] = xp_ref[L + SSD_PAD - (CONV_W - 1):L + SSD_PAD, :]
    xp_ref[0:SSD_PAD, :] = xp_ref[L:L + SSD_PAD, :]
    act = conv * _sigmoid(conv)
    xs = act[:, :SSM_INNER]

    dt_raw = dt_ref[0] + dtb_ref[...]
    dt = jnp.maximum(dt_raw, 0.0) + jnp.log1p(jnp.exp(-jnp.abs(dt_raw)))
    a = -jnp.exp(al_ref[...])
    cs = _cumsum_rows(dt * a)
    cs_t = _transpose_rows(cs)
    dt_t = _transpose_rows(dt)

    tril = lax.broadcasted_iota(jnp.int32, (L, L), 0) >= lax.broadcasted_iota(jnp.int32, (L, L), 1)
    hpg = SSM_HEADS // SSM_GROUPS
    for g in range(SSM_GROUPS):
        b_g = act[:, SSM_INNER + g * SSM_N:SSM_INNER + (g + 1) * SSM_N].astype(BF16)
        c_g = act[:, SSM_INNER + (SSM_GROUPS + g) * SSM_N:SSM_INNER + (SSM_GROUPS + g + 1) * SSM_N].astype(BF16)
        cb = lax.dot_general(c_g, b_g, (((1,), (1,)), ((), ())), preferred_element_type=F32)
        for hh in range(hpg):
            h = g * hpg + hh
            cs_col = cs[:, h:h + 1]
            cs_row = cs_t[h:h + 1, :]
            decay = jnp.exp(jnp.where(tril, cs_col - cs_row, -jnp.inf))
            w = (cb * decay * dt_t[h:h + 1, :]).astype(BF16)
            x_h = xs[:, h * SSM_P:(h + 1) * SSM_P]
            s_h = s_ref[h]
            y_h = jnp.dot(w, x_h.astype(BF16), preferred_element_type=F32)
            y_h = y_h + jnp.exp(cs_col) * lax.dot_general(
                c_g, s_h.astype(BF16), (((1,), (1,)), ((), ())), preferred_element_type=F32)
            ys_ref[:, h * SSM_P:(h + 1) * SSM_P] = y_h
            cs_last = cs_t[h:h + 1, L - 1:L]
            wdec = jnp.exp(cs_last - cs_col) * dt[:, h:h + 1]
            upd = lax.dot_general((x_h * wdec).astype(BF16), b_g, (((0,), (0,)), ((), ())),
                                  preferred_element_type=F32)
            s_ref[h] = jnp.exp(cs_last) * s_h + upd

    z = z_ref[0]
    y = (ys_ref[...] + dsk_ref[...] * xs) * (z * _sigmoid(z))
    gsz = SSM_INNER // SSM_GROUPS
    nw = nw_ref[...]
    for g in range(SSM_GROUPS):
        yg = y[:, g * gsz:(g + 1) * gsz]
        ms = jnp.mean(yg * yg, axis=-1, keepdims=True)
        y_ref[0, :, g * gsz:(g + 1) * gsz] = (yg * lax.rsqrt(ms + EPS) * nw[:, g * gsz:(g + 1) * gsz]).astype(y_ref.dtype)

    @pl.when(c == pl.num_programs(1) - 1)
    def _():
        st_ref[0] = s_ref[...]


def _pad_lanes(v):
    return jnp.pad(v.astype(F32), (0, LANE - v.shape[0])).reshape(1, LANE)


def _ssd(proj, conv_w, conv_b, dt_bias, a_log, d_skip, ssm_norm_w, conv0, s0):
    bsz, t, _ = proj.shape
    L = min(256, t)
    has_state = s0 is not None

    def const(shape):
        return pl.BlockSpec(shape, lambda b, c: (0,) * len(shape))

    in_specs = [pl.BlockSpec((1, L, SSM_INNER), lambda b, c: (b, c, NB_Z * LANE // SSM_INNER)),
                pl.BlockSpec((1, L, CONV_DIM), lambda b, c: (b, c, NB_XBC * LANE // CONV_DIM)),
                pl.BlockSpec((1, L, LANE), lambda b, c: (b, c, NB_DT)),
                const((CONV_W, CONV_DIM)), const((1, CONV_DIM)), const((1, LANE)), const((1, LANE)),
                const((1, SSM_INNER)), const((1, SSM_INNER))]
    args = [proj, proj, proj, conv_w, conv_b.reshape(1, CONV_DIM), _pad_lanes(dt_bias), _pad_lanes(a_log),
            jnp.repeat(d_skip.astype(F32), SSM_P).reshape(1, SSM_INNER), ssm_norm_w.reshape(1, SSM_INNER)]
    if has_state:
        in_specs += [pl.BlockSpec((1, CONV_W - 1, CONV_DIM), lambda b, c: (b, 0, 0)),
                     pl.BlockSpec((1, SSM_HEADS, SSM_P, SSM_N), lambda b, c: (b, 0, 0, 0))]
        args += [conv0, s0]
    return pl.pallas_call(
        functools.partial(_ssd_kernel, L=L, has_state=has_state),
        grid=(bsz, t // L),
        in_specs=in_specs,
        out_specs=[pl.BlockSpec((1, L, SSM_INNER), lambda b, c: (b, c, 0)),
                   pl.BlockSpec((1, SSM_HEADS, SSM_P, SSM_N), lambda b, c: (b, 0, 0, 0)),
                   pl.BlockSpec((1, CONV_W - 1, CONV_DIM), lambda b, c: (b, 0, 0))],
        out_shape=[jax.ShapeDtypeStruct((bsz, t, SSM_INNER), BF16),
                   jax.ShapeDtypeStruct((bsz, SSM_HEADS, SSM_P, SSM_N), F32),
                   jax.ShapeDtypeStruct((bsz, CONV_W - 1, CONV_DIM), F32)],
        scratch_shapes=[pltpu.VMEM((L + SSD_PAD, CONV_DIM), F32),
                        pltpu.VMEM((SSM_HEADS, SSM_P, SSM_N), F32),
                        pltpu.VMEM((L, SSM_INNER), F32)],
        compiler_params=pltpu.CompilerParams(
            dimension_semantics=("parallel", "arbitrary"), vmem_limit_bytes=VMEM_LIMIT),
        name="conv_ssd_scan",
    )(*args)


def _neg_tri_ext(kb):
    kk = lax.broadcasted_iota(jnp.int32, (kb, kb + LANE), 0)
    jj = lax.broadcasted_iota(jnp.int32, (kb, kb + LANE), 1)
    return jnp.where((kk >= jj) | (jj >= kb), -1.0, 0.0).astype(BF16)


def _sb_kernel(*refs, QB, KP, n_past):
    if n_past:
        q_ref, k_ref, v_ref, g_ref, kp_ref, vp_ref, o_ref = refs[:7]
        scr = refs[7:]
    else:
        q_ref, k_ref, v_ref, g_ref, o_ref = refs[:5]
        scr = refs[5:]
    nh = SB_HEADS
    acc_refs, r_refs, xn_ref, tn_ref = scr[0:nh], scr[nh:2 * nh], scr[2 * nh:3 * nh], scr[3 * nh:4 * nh]
    if n_past:
        xp_ref, tp_ref = scr[4 * nh:5 * nh], scr[5 * nh:6 * nh]
    qi = pl.program_id(1)
    nt = (((1,), (1,)), ((), ()))

    def lanes(h):
        return slice(h * SB_DH, (h + 1) * SB_DH)

    def qk(h, k_t):
        return lax.dot_general(q_ref[0, :, lanes(h)], k_t, nt, preferred_element_type=F32)

    def scores(h, z, ntri, x_ref, t_ref, mask=None):
        kb = z.shape[1]
        sp = jnp.maximum(z, 0.0) + jnp.log2(1.0 + jnp.exp2(-jnp.abs(z)))
        if mask is not None:
            sp = jnp.where(mask, sp, 0.0)
        wt = jnp.dot(sp.astype(BF16), ntri, preferred_element_type=F32)
        x = z + wt[:, :kb]
        if mask is not None:
            x = jnp.where(mask, x, -jnp.inf)
        x_ref[h][...] = x
        t_ref[h][...] = wt[:, kb:]

    def accumulate(h, v_t, x_ref, t_ref):
        kb = v_t.shape[0]
        r = r_refs[h][...]
        w = jnp.exp2(x_ref[h][...] + jnp.tile(r, (1, max(kb // LANE, 1)))[:, :kb])
        acc_refs[h][...] += jnp.dot(w.astype(BF16), v_t, preferred_element_type=F32)
        r_refs[h][...] = r + t_ref[h][...]

    for h in range(SB_HEADS):
        acc_refs[h][...] = jnp.zeros_like(acc_refs[h])
        r_refs[h][...] = jnp.zeros_like(r_refs[h])

    start = pl.multiple_of(qi * QB, QB)
    mask = lax.broadcasted_iota(jnp.int32, (QB, QB), 1) < lax.broadcasted_iota(jnp.int32, (QB, QB), 0)
    ntri_q = _neg_tri_ext(QB)
    for h in range(SB_HEADS):
        scores(h, qk(h, k_ref[0, pl.ds(start, QB), lanes(h)]), ntri_q, xn_ref, tn_ref, mask=mask)

    if QB % LANE == 0:
        def new_body(jj, carry):
            s = pl.multiple_of((qi - 1 - jj) * QB, QB)
            zs = [qk(h, k_ref[0, pl.ds(s, QB), lanes(h)]) for h in range(SB_HEADS)]
            for h in range(SB_HEADS):
                accumulate(h, v_ref[0, pl.ds(s + QB, QB), lanes(h)], xn_ref, tn_ref)
            for h in range(SB_HEADS):
                scores(h, zs[h], ntri_q, xn_ref, tn_ref)
            return carry

        lax.fori_loop(0, qi, new_body, 0)

    if n_past:
        ntri_p = _neg_tri_ext(KP)
        last = (n_past - 1) * KP
        for h in range(SB_HEADS):
            accumulate(h, v_ref[0, 0:QB, lanes(h)], xn_ref, tn_ref)
            scores(h, qk(h, kp_ref[0, last:last + KP, lanes(h)].astype(BF16)), ntri_p, xp_ref, tp_ref)

        def past_body(jj, carry):
            s = pl.multiple_of((n_past - 1 - jj) * KP, KP)
            zs = [qk(h, kp_ref[0, pl.ds(s, KP), lanes(h)].astype(BF16)) for h in range(SB_HEADS)]
            for h in range(SB_HEADS):
                accumulate(h, vp_ref[0, pl.ds(s + KP, KP), lanes(h)].astype(BF16), xp_ref, tp_ref)
            for h in range(SB_HEADS):
                scores(h, zs[h], ntri_p, xp_ref, tp_ref)
            return carry

        lax.fori_loop(1, n_past, past_body, 0)
        for h in range(SB_HEADS):
            accumulate(h, vp_ref[0, 0:KP, lanes(h)].astype(BF16), xp_ref, tp_ref)
    else:
        for h in range(SB_HEADS):
            accumulate(h, v_ref[0, 0:QB, lanes(h)], xn_ref, tn_ref)

    gate = g_ref[0]
    for h in range(SB_HEADS):
        gh = gate[:, lanes(h)]
        o_ref[0, :, lanes(h)] = (acc_refs[h][...] * (gh * _sigmoid(gh))).astype(o_ref.dtype)


def _neg_tri_ext_t(kb):
    jj = lax.broadcasted_iota(jnp.int32, (kb + 16, kb), 0)
    kk = lax.broadcasted_iota(jnp.int32, (kb + 16, kb), 1)
    return jnp.where((kk >= jj) | (jj >= kb), -1.0, 0.0).astype(BF16)


def _sbt_kernel(q_ref, k_ref, v_ref, g_ref, o_ref, *scr, QB):
    nh = SB_HEADS
    acc_refs, r_refs, x_refs, t_refs = scr[0:nh], scr[nh:2 * nh], scr[2 * nh:3 * nh], scr[3 * nh:4 * nh]
    qi = pl.program_id(1)
    nt = (((1,), (1,)), ((), ()))
    tn = (((0,), (0,)), ((), ()))
    ntri = _neg_tri_ext_t(QB)

    def lanes(h):
        return slice(h * SB_DH, (h + 1) * SB_DH)

    def qk(h, k_t):
        return lax.dot_general(k_t, q_ref[0, :, lanes(h)], nt, preferred_element_type=F32)

    def scores(h, z, mask=None):
        sp = jnp.maximum(z, 0.0) + jnp.log2(1.0 + jnp.exp2(-jnp.abs(z)))
        if mask is not None:
            sp = jnp.where(mask, sp, 0.0)
        wt = jnp.dot(ntri, sp.astype(BF16), preferred_element_type=F32)
        x = z + wt[:QB]
        if mask is not None:
            x = jnp.where(mask, x, -jnp.inf)
        x_refs[h][...] = x
        t_refs[h][...] = wt[QB:QB + 8]

    def accumulate(h, v_t):
        r = r_refs[h][...]
        w = jnp.exp2(x_refs[h][...] + r[0:1, :])
        acc_refs[h][...] += lax.dot_general(v_t, w.astype(BF16), tn, preferred_element_type=F32)
        r_refs[h][...] = r + t_refs[h][...]

    for h in range(nh):
        acc_refs[h][...] = jnp.zeros_like(acc_refs[h])
        r_refs[h][...] = jnp.zeros_like(r_refs[h])

    start = pl.multiple_of(qi * QB, QB)
    mask = lax.broadcasted_iota(jnp.int32, (QB, QB), 0) < lax.broadcasted_iota(jnp.int32, (QB, QB), 1)
    for h in range(nh):
        scores(h, qk(h, k_ref[0, pl.ds(start, QB), lanes(h)]), mask=mask)

    def body(jj, carry):
        s = pl.multiple_of((qi - 1 - jj) * QB, QB)
        zs = [qk(h, k_ref[0, pl.ds(s, QB), lanes(h)]) for h in range(nh)]
        for h in range(nh):
            accumulate(h, v_ref[0, pl.ds(s + QB, QB), lanes(h)])
        for h in range(nh):
            scores(h, zs[h])
        return carry

    lax.fori_loop(0, qi, body, 0)
    gate = g_ref[0]
    for h in range(nh):
        accumulate(h, v_ref[0, 0:QB, lanes(h)])
        gh = gate[:, lanes(h)]
        o_ref[0, :, lanes(h)] = (acc_refs[h][...].T * (gh * _sigmoid(gh))).astype(o_ref.dtype)


def _stick_breaking(rest, qkv, k_past, v_past, layer):
    bsz, t, _ = rest.shape
    QB = min(256, t)
    KP = 256
    n_past = 0 if k_past is None else k_past.shape[2] // KP
    assert t % QB == 0 and (t == QB or QB % LANE == 0)

    in_specs = [pl.BlockSpec((1, QB, SB_W), lambda b, i: (b, i, 0)),
                pl.BlockSpec((1, t, SB_W), lambda b, i: (b, 0, 1)),
                pl.BlockSpec((1, t, SB_W), lambda b, i: (b, 0, 2)),
                pl.BlockSpec((1, QB, SB_W), lambda b, i: (b, i, NB_SB_G * LANE // SB_W))]
    args = [qkv, qkv, qkv, rest]
    if n_past:
        assert k_past.shape[2] % KP == 0
        past = pl.BlockSpec((None, 1, k_past.shape[2], SB_W), lambda b, i: (layer, b, 0, 0))
        in_specs += [past, past]
        args += [k_past, v_past]
    if not n_past and QB % LANE == 0:
        return pl.pallas_call(
            functools.partial(_sbt_kernel, QB=QB),
            grid=(bsz, t // QB),
            in_specs=in_specs,
            out_specs=pl.BlockSpec((1, QB, SB_W), lambda b, i: (b, i, 0)),
            out_shape=jax.ShapeDtypeStruct((bsz, t, SB_W), BF16),
            scratch_shapes=[pltpu.VMEM((SB_DH, QB), F32)] * SB_HEADS + [pltpu.VMEM((8, QB), F32)] * SB_HEADS
            + [pltpu.VMEM((QB, QB), F32)] * SB_HEADS + [pltpu.VMEM((8, QB), F32)] * SB_HEADS,
            compiler_params=pltpu.CompilerParams(
                dimension_semantics=("parallel", "arbitrary"), vmem_limit_bytes=VMEM_LIMIT),
            name="stick_breaking_t",
        )(*args)
    return pl.pallas_call(
        functools.partial(_sb_kernel, QB=QB, KP=KP, n_past=n_past),
        grid=(bsz, t // QB),
        in_specs=in_specs,
        out_specs=pl.BlockSpec((1, QB, SB_W), lambda b, i: (b, i, 0)),
        out_shape=jax.ShapeDtypeStruct((bsz, t, SB_W), BF16),
        scratch_shapes=[pltpu.VMEM((QB, SB_DH), F32)] * (2 * SB_HEADS) + [pltpu.VMEM((QB, QB), F32)] * SB_HEADS
        + [pltpu.VMEM((QB, LANE), F32)] * SB_HEADS
        + ([pltpu.VMEM((QB, KP), F32)] * SB_HEADS + [pltpu.VMEM((QB, LANE), F32)] * SB_HEADS if n_past else []),
        compiler_params=pltpu.CompilerParams(
            dimension_semantics=("parallel", "arbitrary"), vmem_limit_bytes=VMEM_LIMIT),
        name="stick_breaking",
    )(*args)


def _outproj_kernel(*refs, final):
    if final:
        x_ref, a_ref, b_ref, c_ref, w_ref, fw_ref, o_ref = refs
    else:
        x_ref, a_ref, b_ref, c_ref, w_ref, o_ref = refs
    y = jnp.dot(a_ref[...], w_ref[0:HG_W, :], preferred_element_type=F32)
    y = y + jnp.dot(b_ref[...], w_ref[HG_W:HG_W + SSM_INNER, :], preferred_element_type=F32)
    y = y + jnp.dot(c_ref[...], w_ref[HG_W + SSM_INNER:, :], preferred_element_type=F32)
    x = x_ref[...] + y
    if final:
        ms = jnp.mean(x * x, axis=-1, keepdims=True)
        x = x * lax.rsqrt(ms + EPS) * fw_ref[...]
    o_ref[...] = x


def _out_proj(x2d, o_hg, y_ssm, o_sb, w_bf, final_w):
    m, d = x2d.shape
    tm = min(1024, m)
    final = final_w is not None

    def rows(w):
        return pl.BlockSpec((tm, w), lambda i: (i, 0))

    in_specs = [rows(d), rows(HG_W), rows(SSM_INNER), rows(SB_W),
                pl.BlockSpec(w_bf.shape, lambda i: (0, 0))]
    args = [x2d, o_hg, y_ssm, o_sb, w_bf]
    if final:
        in_specs.append(pl.BlockSpec((1, d), lambda i: (0, 0)))
        args.append(final_w.reshape(1, d))
    return pl.pallas_call(
        functools.partial(_outproj_kernel, final=final),
        grid=(m // tm,),
        in_specs=in_specs,
        out_specs=rows(d),
        out_shape=jax.ShapeDtypeStruct((m, d), F32),
        compiler_params=pltpu.CompilerParams(
            dimension_semantics=("parallel",), vmem_limit_bytes=VMEM_LIMIT),
        name="out_proj",
    )(*args)


def _reorder_w_in(w_in):
    hg_cols = 4 * HG_W
    ssm_cols = SSM_INNER + CONV_DIM + SSM_HEADS
    hg = w_in[..., :hg_cols]
    z_xbc = w_in[..., hg_cols:hg_cols + SSM_INNER + CONV_DIM]
    dt = w_in[..., hg_cols + SSM_INNER + CONV_DIM:hg_cols + ssm_cols]
    sb = w_in[..., hg_cols + ssm_cols:]
    sb_q = sb[..., :SB_W] * (LOG2E * SB_DH ** -0.5)
    sb_kv = sb[..., SB_W:3 * SB_W]
    sb_g = sb[..., 3 * SB_W:]
    used = w_in.shape[-1]
    pad = jnp.zeros(w_in.shape[:-1] + (NP - used,), w_in.dtype)
    return jnp.concatenate([sb_q, sb_kv, hg, z_xbc, sb_g, dt, pad], axis=-1).astype(BF16)


def _lower_bounds(lb_logits):
    p = jax.nn.softmax(lb_logits.astype(F32), axis=0)
    return jnp.clip(jnp.cumsum(p, axis=0) - p[0:1], 0.0, 1.0)


def _layer(x, states, kv_all, layer, depth, lb, norm_w, w_in_bf, conv_w, conv_b, dt_bias, a_log, d_skip,
           ssm_norm_w, hg_norm_w, w_out_bf, final_w):
    bsz, t, d = x.shape
    conv0, ssm0, hg0, k_past, v_past = states
    x2d = x.reshape(bsz * t, d)
    rest, qkv, k_all, v_all = _in_proj(x2d, norm_w, w_in_bf, layer, depth, *kv_all)
    rest = rest.reshape(bsz, t, NR)
    qkv = qkv.reshape(bsz, t, QKV_TILES * PROJ_TN)
    o_hg, hg_state = _hgrn(rest, lb, hg_norm_w, hg0)
    y_ssm, ssm_state, conv_state = _ssd(rest, conv_w, conv_b, dt_bias, a_log, d_skip, ssm_norm_w, conv0, ssm0)
    o_sb = _stick_breaking(rest, qkv, k_past, v_past, layer)
    x_new = _out_proj(x2d, o_hg.reshape(bsz * t, HG_W), y_ssm.reshape(bsz * t, SSM_INNER),
                      o_sb.reshape(bsz * t, SB_W), w_out_bf, final_w).reshape(bsz, t, d)
    return x_new, (hg_state, ssm_state, conv_state), (k_all, v_all)


def _run_group(x, layer_states, lbs, w_in_bf, w_out_bf, params, final_norm_w):
    norm_w, conv_w, conv_b, dt_bias, a_log, d_skip, ssm_norm_w, hg_norm_w = params
    depth = norm_w.shape[0]
    bsz, t, _ = x.shape
    outs = []
    kv_all = (None, None)
    for l in range(depth):
        x, st, kv_all = _layer(x, layer_states(l), kv_all, l, depth, lbs[l], norm_w[l], w_in_bf[l], conv_w[l],
                               conv_b[l], dt_bias[l], a_log[l], d_skip[l], ssm_norm_w[l], hg_norm_w[l],
                               w_out_bf[l], final_norm_w if l == depth - 1 else None)
        outs.append(st)
    kv = tuple(a.reshape(depth, bsz, t, SB_HEADS, SB_DH) for a in kv_all)
    return x, tuple(jnp.stack([st[i] for st in outs]) for i in range(3)) + kv


def kernel(x_prompt, x_sample, state_hgrn, state_ssm, state_conv, cache_k, cache_v, norm_w, w_in, lb_logits,
           conv_w, conv_b, dt_bias, a_log, d_skip, ssm_norm_w, hg_norm_w, w_out, final_norm_w):
    lbs = _lower_bounds(lb_logits)
    w_in_bf = _reorder_w_in(w_in)
    w_out_bf = w_out.astype(BF16)
    params = (norm_w, conv_w, conv_b, dt_bias, a_log, d_skip, ssm_norm_w, hg_norm_w)

    y_prompt, p_st = _run_group(x_prompt, lambda l: (None, None, None, None, None),
                                lbs, w_in_bf, w_out_bf, params, final_norm_w)
    k_cache = cache_k.reshape(cache_k.shape[:3] + (SB_W,))
    v_cache = cache_v.reshape(cache_v.shape[:3] + (SB_W,))
    y_sample, s_st = _run_group(x_sample,
                                lambda l: (state_conv[l], state_ssm[l], state_hgrn[l], k_cache, v_cache),
                                lbs, w_in_bf, w_out_bf, params, final_norm_w)
    return (y_prompt, y_sample) + p_st + s_st
```

```python
import functools

import jax
import jax.numpy as jnp
from jax import lax
from jax.experimental import pallas as pl
from jax.experimental.pallas import tpu as pltpu

F32 = jnp.float32
BF16 = jnp.bfloat16
EPS = 1e-6

LANE = 128
HG_HEADS = 4
HG_D = 128
SSM_HEADS = 16
SSM_P = 64
SSM_N = 128
SSM_GROUPS = 2
SSM_INNER = SSM_HEADS * SSM_P
CONV_W = 4
CONV_DIM = SSM_INNER + 2 * SSM_GROUPS * SSM_N
SB_HEADS = 4
SB_DH = 128
HG_W = HG_HEADS * HG_D
SB_W = SB_HEADS * SB_DH

PROJ_TN = 8 * LANE
QKV_TILES = 2
QKV_W = QKV_TILES * PROJ_TN
NB_XBC = 0
NB_DT = 12
NB_Z = 16
NB_HG_Q, NB_HG_F, NB_HG_I, NB_HG_G = 24, 28, 32, 36
NB_SB_G = 44
REST_TILES = 6
NR = REST_TILES * PROJ_TN
NP = 7 * PROJ_TN
LOG2E = 1.4426950408889634

VMEM_LIMIT = 56 * 1024 * 1024

SSD_PAD = 8


def _sigmoid(x):
    return 1.0 / (1.0 + jnp.exp(-x))


def _cumsum_rows_scan(x):
    n = x.shape[0]
    row = lax.broadcasted_iota(jnp.int32, x.shape, 0)
    sh = 1
    while sh < n:
        x = x + jnp.where(row >= sh, pltpu.roll(x, sh, axis=0), 0.0)
        sh *= 2
    return x


def _split3(x):
    hi = x.astype(BF16)
    r1 = x - hi.astype(F32)
    mid = r1.astype(BF16)
    lo = (r1 - mid.astype(F32)).astype(BF16)
    return hi, mid, lo


def _cumsum_rows(x):
    n = x.shape[0]
    tri = (lax.broadcasted_iota(jnp.int32, (n, n), 0) >= lax.broadcasted_iota(jnp.int32, (n, n), 1))
    tri = jnp.where(tri, 1.0, 0.0).astype(BF16)
    hi, mid, lo = _split3(x)
    out = jnp.dot(tri, lo, preferred_element_type=F32)
    out = out + jnp.dot(tri, mid, preferred_element_type=F32)
    return out + jnp.dot(tri, hi, preferred_element_type=F32)


def _transpose_rows(x):
    n = x.shape[0]
    if n < LANE:
        x = jnp.concatenate([x, jnp.zeros((LANE - n, LANE), x.dtype)], axis=0)
    return x.T[:, :n]


def _inproj_kernel(*refs, aliased):
    if aliased:
        x_ref, nw_ref, w_ref, _, _, rest_ref, qkv_ref, k_ref, v_ref, xn_ref = refs
    else:
        x_ref, nw_ref, w_ref, rest_ref, qkv_ref, k_ref, v_ref, xn_ref = refs
    j = pl.program_id(1)

    @pl.when(j == 0)
    def _():
        x = x_ref[...]
        ms = jnp.mean(x * x, axis=-1, keepdims=True)
        xn_ref[...] = (x * lax.rsqrt(ms + EPS) * nw_ref[...]).astype(BF16)

    acc = jnp.dot(xn_ref[...], w_ref[...], preferred_element_type=F32)

    @pl.when(j < QKV_TILES)
    def _():
        qkv_ref[...] = acc.astype(BF16)

    @pl.when(j == 0)
    def _():
        k_ref[0] = acc[:, SB_W:]

    @pl.when(j == 1)
    def _():
        v_ref[0] = acc[:, :SB_W]

    @pl.when(j >= 1)
    def _():
        rest_ref[...] = acc


def _in_proj(x2d, norm_w, w_bf, layer, depth, k_all, v_all):
    m, d = x2d.shape
    tm = min(1024, m)
    aliased = k_all is not None
    in_specs = [
        pl.BlockSpec((tm, d), lambda i, j: (i, 0)),
        pl.BlockSpec((1, d), lambda i, j: (0, 0)),
        pl.BlockSpec((d, PROJ_TN), lambda i, j: (0, j)),
    ]
    args = [x2d, norm_w.reshape(1, d), w_bf]
    if aliased:
        in_specs += [pl.BlockSpec(memory_space=pl.ANY), pl.BlockSpec(memory_space=pl.ANY)]
        args += [k_all, v_all]
    kv_spec = pl.BlockSpec((1, tm, SB_W), lambda i, j: (layer, i, 0))
    kv_shape = jax.ShapeDtypeStruct((depth, m, SB_W), F32)
    return pl.pallas_call(
        functools.partial(_inproj_kernel, aliased=aliased),
        grid=(m // tm, NP // PROJ_TN),
        in_specs=in_specs,
        out_specs=[pl.BlockSpec((tm, PROJ_TN), lambda i, j: (i, jnp.where(j < QKV_TILES, REST_TILES - 1, j - QKV_TILES))),
                   pl.BlockSpec((tm, PROJ_TN), lambda i, j: (i, jnp.minimum(j, QKV_TILES - 1))),
                   kv_spec, kv_spec],
        out_shape=[jax.ShapeDtypeStruct((m, NR), F32),
                   jax.ShapeDtypeStruct((m, QKV_W), BF16),
                   kv_shape, kv_shape],
        input_output_aliases={3: 2, 4: 3} if aliased else {},
        scratch_shapes=[pltpu.VMEM((tm, d), BF16)],
        compiler_params=pltpu.CompilerParams(
            dimension_semantics=("parallel", "arbitrary"), vmem_limit_bytes=VMEM_LIMIT),
        name="in_proj",
    )(*args)


def _hgrn_kernel(*refs, C, has_state):
    if has_state:
        q_ref, f_ref, i_ref, g_ref, lb_ref, nw_ref, s0_ref, o_ref, st_ref = refs[:9]
        scr = refs[9:]
    else:
        q_ref, f_ref, i_ref, g_ref, lb_ref, nw_ref, o_ref, st_ref = refs[:8]
        scr = refs[8:]
    sT_refs, b_refs = scr[:HG_HEADS], scr[HG_HEADS:]
    c = pl.program_id(1)

    @pl.when(c == 0)
    def _():
        for h in range(HG_HEADS):
            if has_state:
                sT_refs[h][...] = s0_ref[0, h].T
            else:
                sT_refs[h][...] = jnp.zeros((HG_D, HG_D), F32)

    for h in range(HG_HEADS):
        _hgrn_head(h, q_ref, f_ref, i_ref, g_ref, lb_ref, nw_ref, o_ref, sT_refs[h], b_refs[h], C)

    @pl.when(c == pl.num_programs(1) - 1)
    def _():
        for h in range(HG_HEADS):
            st_ref[0, h] = sT_refs[h][...].T


def _hgrn_head(head, q_ref, f_ref, i_ref, g_ref, lb_ref, nw_ref, o_ref, sT_ref, b_ref, C):
    cols = slice(head * HG_D, (head + 1) * HG_D)
    hq = q_ref[0, :, cols]
    hf = f_ref[0, :, cols]
    v = i_ref[0, :, cols]
    gate = g_ref[0, :, cols]
    lb = lb_ref[:, cols]

    q = hq * _sigmoid(hq)
    e = jnp.exp(-jnp.abs(hf))
    la = jnp.log(lb)
    lc = jnp.log1p(-lb) + (jnp.minimum(hf, 0.0) - jnp.log(1.0 + e))
    log_f = jnp.maximum(la, lc) + jnp.log(1.0 + jnp.exp(-jnp.abs(la - lc)))
    k = (1.0 - lb) * (jnp.where(hf >= 0.0, e, 1.0) / (1.0 + e))

    b = _cumsum_rows_scan(log_f)
    b_ref[...] = b
    row = lax.broadcasted_iota(jnp.int32, (C, HG_D), 0)

    att = jnp.zeros((C, C), F32)
    n = C
    while n >= 16:
        h = n // 2
        pieces = [jnp.broadcast_to(b_ref[blk * n + h - 1:blk * n + h, :], (n, HG_D)) for blk in range(C // n)]
        b_mid = pieces[0] if len(pieces) == 1 else jnp.concatenate(pieces, axis=0)
        second = (row & (n - 1)) >= h
        ex = jnp.exp(jnp.where(second, b - b_mid, b_mid - b))
        q_n = jnp.where(second, q * ex, 0.0).astype(BF16)
        k_n = jnp.where(second, 0.0, k * ex).astype(BF16)
        a_n = lax.dot_general(q_n, k_n, (((1,), (1,)), ((), ())), preferred_element_type=F32)
        if n < C:
            shift = n.bit_length() - 1
            rb = lax.broadcasted_iota(jnp.int32, (C, C), 0) >> shift
            cb = lax.broadcasted_iota(jnp.int32, (C, C), 1) >> shift
            a_n = jnp.where(rb == cb, a_n, 0.0)
        att = att + a_n
        n = h

    ones = jnp.ones((HG_D, HG_D), BF16)
    r8 = row & 7
    o = jnp.zeros((C, HG_D), F32)
    for d in range(8):
        if d == 0:
            p = q * k
            vs = v
        else:
            ks = pltpu.roll(k, d, axis=0)
            bs = pltpu.roll(b, d, axis=0)
            vs = pltpu.roll(v, d, axis=0)
            p = jnp.where(r8 >= d, q * ks * jnp.exp(b - bs), 0.0)
        a_d = jnp.dot(p.astype(BF16), ones, preferred_element_type=F32)
        o = o + a_d * vs

    s_t = sT_ref[...]
    v_bf = v.astype(BF16)
    o = o + lax.dot_general((q * jnp.exp(b)).astype(BF16), s_t.astype(BF16),
                            (((1,), (1,)), ((), ())), preferred_element_type=F32)
    o = o + jnp.dot(att.astype(BF16), v_bf, preferred_element_type=F32)

    b_last = b_ref[C - 1:C, :]
    k_dec = (k * jnp.exp(b_last - b)).astype(BF16)
    upd = lax.dot_general(v_bf, k_dec, (((0,), (0,)), ((), ())), preferred_element_type=F32)
    sT_ref[...] = s_t * jnp.exp(b_last) + upd

    ms = jnp.mean(o * o, axis=-1, keepdims=True)
    o = o * lax.rsqrt(ms + EPS) * nw_ref[...]
    o_ref[0, :, cols] = (o * (gate * _sigmoid(gate))).astype(o_ref.dtype)


def _hgrn(proj, lb, nw, s0):
    bsz, t, _ = proj.shape
    C = min(128, t)
    has_state = s0 is not None

    def col(nb):
        return pl.BlockSpec((1, C, HG_W), lambda b, c: (b, c, nb // HG_HEADS))

    in_specs = [col(NB_HG_Q), col(NB_HG_F), col(NB_HG_I), col(NB_HG_G),
                pl.BlockSpec((1, HG_W), lambda b, c: (0, 0)),
                pl.BlockSpec((1, HG_D), lambda b, c: (0, 0))]
    args = [proj, proj, proj, proj, lb.reshape(1, HG_W), nw.reshape(1, HG_D)]
    if has_state:
        in_specs.append(pl.BlockSpec((1, HG_HEADS, HG_D, HG_D), lambda b, c: (b, 0, 0, 0)))
        args.append(s0)
    return pl.pallas_call(
        functools.partial(_hgrn_kernel, C=C, has_state=has_state),
        grid=(bsz, t // C),
        in_specs=in_specs,
        out_specs=[pl.BlockSpec((1, C, HG_W), lambda b, c: (b, c, 0)),
                   pl.BlockSpec((1, HG_HEADS, HG_D, HG_D), lambda b, c: (b, 0, 0, 0))],
        out_shape=[jax.ShapeDtypeStruct((bsz, t, HG_W), BF16),
                   jax.ShapeDtypeStruct((bsz, HG_HEADS, HG_D, HG_D), F32)],
        scratch_shapes=[pltpu.VMEM((HG_D, HG_D), F32)] * HG_HEADS + [pltpu.VMEM((C, HG_D), F32)] * HG_HEADS,
        compiler_params=pltpu.CompilerParams(
            dimension_semantics=("parallel", "arbitrary"), vmem_limit_bytes=VMEM_LIMIT),
        name="hgrn2_scan",
    )(*args)


def _ssd_kernel(*refs, L, has_state):
    if has_state:
        (z_ref, xbc_ref, dt_ref, cw_ref, cb_ref, dtb_ref, al_ref, dsk_ref, nw_ref, conv0_ref, s0_ref,
         y_ref, st_ref, cst_ref, xp_ref, s_ref, ys_ref) = refs
    else:
        (z_ref, xbc_ref, dt_ref, cw_ref, cb_ref, dtb_ref, al_ref, dsk_ref, nw_ref,
         y_ref, st_ref, cst_ref, xp_ref, s_ref, ys_ref) = refs
    c = pl.program_id(1)

    @pl.when(c == 0)
    def _():
        xp_ref[0:SSD_PAD, :] = jnp.zeros((SSD_PAD, CONV_DIM), F32)
        if has_state:
            xp_ref[SSD_PAD - (CONV_W - 1):SSD_PAD, :] = conv0_ref[0]
            s_ref[...] = s0_ref[0]
        else:
            s_ref[...] = jnp.zeros((SSM_HEADS, SSM_P, SSM_N), F32)

    xp_ref[SSD_PAD:SSD_PAD + L, :] = xbc_ref[0]
    cw = cw_ref[...]
    conv = cb_ref[...]
    for j in range(CONV_W):
        lo = SSD_PAD - (CONV_W - 1) + j
        conv = conv + cw[j:j + 1, :] * xp_ref[lo:lo + L, :]
    cst_ref[0] = xp_ref[L + SSD_PAD - (CONV_W - 1):L + SSD_PAD, :]
    xp_ref[0:SSD_PAD, :] = xp_ref[L:L + SSD_PAD, :]
    act = conv * _sigmoid(conv)
    xs = act[:, :SSM_INNER]

    dt_raw = dt_ref[0] + dtb_ref[...]
    dt = jnp.maximum(dt_raw, 0.0) + jnp.log(1.0 + jnp.exp(-jnp.abs(dt_raw)))
    a = -jnp.exp(al_ref[...])
    cs = _cumsum_rows(dt * a)
    cs_t = _transpose_rows(cs)
    dt_t = _transpose_rows(dt)

    tril = lax.broadcasted_iota(jnp.int32, (L, L), 0) >= lax.broadcasted_iota(jnp.int32, (L, L), 1)
    hpg = SSM_HEADS // SSM_GROUPS
    for g in range(SSM_GROUPS):
        b_g = act[:, SSM_INNER + g * SSM_N:SSM_INNER + (g + 1) * SSM_N].astype(BF16)
        c_g = act[:, SSM_INNER + (SSM_GROUPS + g) * SSM_N:SSM_INNER + (SSM_GROUPS + g + 1) * SSM_N].astype(BF16)
        cb = lax.dot_general(c_g, b_g, (((1,), (1,)), ((), ())), preferred_element_type=F32)
        for hh in range(hpg):
            h = g * hpg + hh
            cs_col = cs[:, h:h + 1]
            cs_row = cs_t[h:h + 1, :]
            decay = jnp.exp(jnp.where(tril, cs_col - cs_row, -jnp.inf))
            w = (cb * decay * dt_t[h:h + 1, :]).astype(BF16)
            x_h = xs[:, h * SSM_P:(h + 1) * SSM_P]
            s_h = s_ref[h]
            y_h = jnp.dot(w, x_h.astype(BF16), preferred_element_type=F32)
            y_h = y_h + jnp.exp(cs_col) * lax.dot_general(
                c_g, s_h.astype(BF16), (((1,), (1,)), ((), ())), preferred_element_type=F32)
            ys_ref[:, h * SSM_P:(h + 1) * SSM_P] = y_h
            cs_last = cs_t[h:h + 1, L - 1:L]
            wdec = jnp.exp(cs_last - cs_col) * dt[:, h:h + 1]
            upd = lax.dot_general((x_h * wdec).astype(BF16), b_g, (((0,), (0,)), ((), ())),
                                  preferred_element_type=F32)
            s_ref[h] = jnp.exp(cs_last) * s_h + upd

    z = z_ref[0]
    y = (ys_ref[...] + dsk_ref[...] * xs) * (z * _sigmoid(z))
    gsz = SSM_INNER // SSM_GROUPS
    nw = nw_ref[...]
    for g in range(SSM_GROUPS):
        yg = y[:, g * gsz:(g + 1) * gsz]
        ms = jnp.mean(yg * yg, axis=-1, keepdims=True)
        y_ref[0, :, g * gsz:(g + 1) * gsz] = (yg * lax.rsqrt(ms + EPS) * nw[:, g * gsz:(g + 1) * gsz]).astype(y_ref.dtype)

    @pl.when(c == pl.num_programs(1) - 1)
    def _():
        st_ref[0] = s_ref[...]


def _pad_lanes(v):
    return jnp.pad(v.astype(F32), (0, LANE - v.shape[0])).reshape(1, LANE)


def _ssd(proj, conv_w, conv_b, dt_bias, a_log, d_skip, ssm_norm_w, conv0, s0):
    bsz, t, _ = proj.shape
    L = min(256, t)
    has_state = s0 is not None

    def const(shape):
        return pl.BlockSpec(shape, lambda b, c: (0,) * len(shape))

    in_specs = [pl.BlockSpec((1, L, SSM_INNER), lambda b, c: (b, c, NB_Z * LANE // SSM_INNER)),
                pl.BlockSpec((1, L, CONV_DIM), lambda b, c: (b, c, NB_XBC * LANE // CONV_DIM)),
                pl.BlockSpec((1, L, LANE), lambda b, c: (b, c, NB_DT)),
                const((CONV_W, CONV_DIM)), const((1, CONV_DIM)), const((1, LANE)), const((1, LANE)),
                const((1, SSM_INNER)), const((1, SSM_INNER))]
    args = [proj, proj, proj, conv_w, conv_b.reshape(1, CONV_DIM), _pad_lanes(dt_bias), _pad_lanes(a_log),
            jnp.repeat(d_skip.astype(F32), SSM_P).reshape(1, SSM_INNER), ssm_norm_w.reshape(1, SSM_INNER)]
    if has_state:
        in_specs += [pl.BlockSpec((1, CONV_W - 1, CONV_DIM), lambda b, c: (b, 0, 0)),
                     pl.BlockSpec((1, SSM_HEADS, SSM_P, SSM_N), lambda b, c: (b, 0, 0, 0))]
        args += [conv0, s0]
    return pl.pallas_call(
        functools.partial(_ssd_kernel, L=L, has_state=has_state),
        grid=(bsz, t // L),
        in_specs=in_specs,
        out_specs=[pl.BlockSpec((1, L, SSM_INNER), lambda b, c: (b, c, 0)),
                   pl.BlockSpec((1, SSM_HEADS, SSM_P, SSM_N), lambda b, c: (b, 0, 0, 0)),
                   pl.BlockSpec((1, CONV_W - 1, CONV_DIM), lambda b, c: (b, 0, 0))],
        out_shape=[jax.ShapeDtypeStruct((bsz, t, SSM_INNER), BF16),
                   jax.ShapeDtypeStruct((bsz, SSM_HEADS, SSM_P, SSM_N), F32),
                   jax.ShapeDtypeStruct((bsz, CONV_W - 1, CONV_DIM), F32)],
        scratch_shapes=[pltpu.VMEM((L + SSD_PAD, CONV_DIM), F32),
                        pltpu.VMEM((SSM_HEADS, SSM_P, SSM_N), F32),
                        pltpu.VMEM((L, SSM_INNER), F32)],
        compiler_params=pltpu.CompilerParams(
            dimension_semantics=("parallel", "arbitrary"), vmem_limit_bytes=VMEM_LIMIT),
        name="conv_ssd_scan",
    )(*args)


def _neg_tri_ext(kb):
    kk = lax.broadcasted_iota(jnp.int32, (kb, kb + LANE), 0)
    jj = lax.broadcasted_iota(jnp.int32, (kb, kb + LANE), 1)
    return jnp.where((kk >= jj) | (jj >= kb), -1.0, 0.0).astype(BF16)


def _sb_kernel(*refs, QB, KP, n_past):
    if n_past:
        q_ref, k_ref, v_ref, g_ref, kp_ref, vp_ref, o_ref = refs[:7]
        scr = refs[7:]
    else:
        q_ref, k_ref, v_ref, g_ref, o_ref = refs[:5]
        scr = refs[5:]
    nh = SB_HEADS
    acc_refs, r_refs, xn_ref, tn_ref = scr[0:nh], scr[nh:2 * nh], scr[2 * nh:3 * nh], scr[3 * nh:4 * nh]
    if n_past:
        xp_ref, tp_ref = scr[4 * nh:5 * nh], scr[5 * nh:6 * nh]
    qi = pl.program_id(1)
    nt = (((1,), (1,)), ((), ()))

    def lanes(h):
        return slice(h * SB_DH, (h + 1) * SB_DH)

    def qk(h, k_t):
        return lax.dot_general(q_ref[0, :, lanes(h)], k_t, nt, preferred_element_type=F32)

    def scores(h, z, ntri, x_ref, t_ref, mask=None):
        kb = z.shape[1]
        sp = jnp.maximum(z, 0.0) + jnp.log2(1.0 + jnp.exp2(-jnp.abs(z)))
        if mask is not None:
            sp = jnp.where(mask, sp, 0.0)
        wt = jnp.dot(sp.astype(BF16), ntri, preferred_element_type=F32)
        x = z + wt[:, :kb]
        if mask is not None:
            x = jnp.where(mask, x, -jnp.inf)
        x_ref[h][...] = x
        t_ref[h][...] = wt[:, kb:]

    def accumulate(h, v_t, x_ref, t_ref):
        kb = v_t.shape[0]
        r = r_refs[h][...]
        w = jnp.exp2(x_ref[h][...] + jnp.tile(r, (1, max(kb // LANE, 1)))[:, :kb])
        acc_refs[h][...] += jnp.dot(w.astype(BF16), v_t, preferred_element_type=F32)
        r_refs[h][...] = r + t_ref[h][...]

    for h in range(SB_HEADS):
        acc_refs[h][...] = jnp.zeros_like(acc_refs[h])
        r_refs[h][...] = jnp.zeros_like(r_refs[h])

    start = pl.multiple_of(qi * QB, QB)
    mask = lax.broadcasted_iota(jnp.int32, (QB, QB), 1) < lax.broadcasted_iota(jnp.int32, (QB, QB), 0)
    ntri_q = _neg_tri_ext(QB)
    for h in range(SB_HEADS):
        scores(h, qk(h, k_ref[0, pl.ds(start, QB), lanes(h)]), ntri_q, xn_ref, tn_ref, mask=mask)

    if QB % LANE == 0:
        def new_body(jj, carry):
            s = pl.multiple_of((qi - 1 - jj) * QB, QB)
            zs = [qk(h, k_ref[0, pl.ds(s, QB), lanes(h)]) for h in range(SB_HEADS)]
            for h in range(SB_HEADS):
                accumulate(h, v_ref[0, pl.ds(s + QB, QB), lanes(h)], xn_ref, tn_ref)
            for h in range(SB_HEADS):
                scores(h, zs[h], ntri_q, xn_ref, tn_ref)
            return carry

        lax.fori_loop(0, qi, new_body, 0)

    if n_past:
        ntri_p = _neg_tri_ext(KP)
        last = (n_past - 1) * KP
        for h in range(SB_HEADS):
            accumulate(h, v_ref[0, 0:QB, lanes(h)], xn_ref, tn_ref)
            scores(h, qk(h, kp_ref[0, last:last + KP, lanes(h)].astype(BF16)), ntri_p, xp_ref, tp_ref)

        def past_body(jj, carry):
            s = pl.multiple_of((n_past - 1 - jj) * KP, KP)
            zs = [qk(h, kp_ref[0, pl.ds(s, KP), lanes(h)].astype(BF16)) for h in range(SB_HEADS)]
            for h in range(SB_HEADS):
                accumulate(h, vp_ref[0, pl.ds(s + KP, KP), lanes(h)].astype(BF16), xp_ref, tp_ref)
            for h in range(SB_HEADS):
                scores(h, zs[h], ntri_p, xp_ref, tp_ref)
            return carry

        lax.fori_loop(1, n_past, past_body, 0)
        for h in range(SB_HEADS):
            accumulate(h, vp_ref[0, 0:KP, lanes(h)].astype(BF16), xp_ref, tp_ref)
    else:
        for h in range(SB_HEADS):
            accumulate(h, v_ref[0, 0:QB, lanes(h)], xn_ref, tn_ref)

    gate = g_ref[0]
    for h in range(SB_HEADS):
        gh = gate[:, lanes(h)]
        o_ref[0, :, lanes(h)] = (acc_refs[h][...] * (gh * _sigmoid(gh))).astype(o_ref.dtype)


def _neg_tri_ext_t(kb):
    jj = lax.broadcasted_iota(jnp.int32, (kb + 16, kb), 0)
    kk = lax.broadcasted_iota(jnp.int32, (kb + 16, kb), 1)
    return jnp.where((kk >= jj) | (jj >= kb), -1.0, 0.0).astype(BF16)


def _sbt_kernel(q_ref, k_ref, v_ref, g_ref, o_ref, *scr, QB):
    nh = SB_HEADS
    acc_refs, r_refs, z_refs, sp_refs, x_refs, t_refs = (scr[i * nh:(i + 1) * nh] for i in range(6))
    qi = pl.program_id(1)
    nt = (((1,), (1,)), ((), ()))
    tn = (((0,), (0,)), ((), ()))
    ntri = _neg_tri_ext_t(QB)

    def lanes(h):
        return slice(h * SB_DH, (h + 1) * SB_DH)

    def stage_a_mm(tile, slot):
        s = pl.multiple_of(tile * QB, QB)
        for h in range(nh):
            z_refs[h][slot] = lax.dot_general(k_ref[0, pl.ds(s, QB), lanes(h)], q_ref[0, :, lanes(h)], nt,
                                              preferred_element_type=F32)

    def stage_a_elt(slot, mask=None):
        for h in range(nh):
            z = z_refs[h][slot]
            sp = jnp.maximum(z, 0.0) + jnp.log2(1.0 + jnp.exp2(-jnp.abs(z)))
            if mask is not None:
                sp = jnp.where(mask, sp, 0.0)
                z_refs[h][slot] = jnp.where(mask, z, -jnp.inf)
            sp_refs[h][...] = sp.astype(BF16)

    def stage_b(slot):
        for h in range(nh):
            wt = jnp.dot(ntri, sp_refs[h][...], preferred_element_type=F32)
            x_refs[h][...] = z_refs[h][slot] + wt[:QB]
            t_refs[h][...] = wt[QB:QB + 8]

    def stage_c(tile):
        s = pl.multiple_of(tile * QB, QB)
        for h in range(nh):
            r = r_refs[h][...]
            w = jnp.exp2(x_refs[h][...] + r[0:1, :])
            acc_refs[h][...] += lax.dot_general(v_ref[0, pl.ds(s, QB), lanes(h)], w.astype(BF16), tn,
                                                preferred_element_type=F32)
            r_refs[h][...] = r + t_refs[h][...]

    for h in range(nh):
        acc_refs[h][...] = jnp.zeros_like(acc_refs[h])
        r_refs[h][...] = jnp.zeros_like(r_refs[h])

    mask = lax.broadcasted_iota(jnp.int32, (QB, QB), 0) < lax.broadcasted_iota(jnp.int32, (QB, QB), 1)
    stage_a_mm(qi, 0)
    stage_a_elt(0, mask=mask)

    @pl.when(qi >= 1)
    def _():
        stage_a_mm(qi - 1, 1)
        stage_b(0)
        stage_a_elt(1)

    def body(i, carry):
        slot = i & 1
        stage_a_mm(qi - i, slot)
        stage_c(qi - i + 2)
        stage_b(1 - slot)
        stage_a_elt(slot)
        return carry

    lax.fori_loop(2, qi + 1, body, 0)

    @pl.when(qi >= 1)
    def _():
        stage_c(1)

    stage_b(qi & 1)
    stage_c(0)
    gate = g_ref[0]
    for h in range(nh):
        gh = gate[:, lanes(h)]
        o_ref[0, :, lanes(h)] = (acc_refs[h][...].T * (gh * _sigmoid(gh))).astype(o_ref.dtype)


def _stick_breaking(rest, qkv, k_past, v_past, layer):
    bsz, t, _ = rest.shape
    QB = min(256, t)
    KP = 256
    n_past = 0 if k_past is None else k_past.shape[2] // KP
    assert t % QB == 0 and (t == QB or QB % LANE == 0)

    in_specs = [pl.BlockSpec((1, QB, SB_W), lambda b, i: (b, i, 0)),
                pl.BlockSpec((1, t, SB_W), lambda b, i: (b, 0, 1)),
                pl.BlockSpec((1, t, SB_W), lambda b, i: (b, 0, 2)),
                pl.BlockSpec((1, QB, SB_W), lambda b, i: (b, i, NB_SB_G * LANE // SB_W))]
    args = [qkv, qkv, qkv, rest]
    if n_past:
        assert k_past.shape[2] % KP == 0
        past = pl.BlockSpec((None, 1, k_past.shape[2], SB_W), lambda b, i: (layer, b, 0, 0))
        in_specs += [past, past]
        args += [k_past, v_past]
    if not n_past and QB % LANE == 0:
        return pl.pallas_call(
            functools.partial(_sbt_kernel, QB=QB),
            grid=(bsz, t // QB),
            in_specs=in_specs,
            out_specs=pl.BlockSpec((1, QB, SB_W), lambda b, i: (b, i, 0)),
            out_shape=jax.ShapeDtypeStruct((bsz, t, SB_W), BF16),
            scratch_shapes=[pltpu.VMEM((SB_DH, QB), F32)] * SB_HEADS + [pltpu.VMEM((8, QB), F32)] * SB_HEADS
            + [pltpu.VMEM((2, QB, QB), F32)] * SB_HEADS + [pltpu.VMEM((QB, QB), BF16)] * SB_HEADS
            + [pltpu.VMEM((QB, QB), F32)] * SB_HEADS + [pltpu.VMEM((8, QB), F32)] * SB_HEADS,
            compiler_params=pltpu.CompilerParams(
                dimension_semantics=("parallel", "arbitrary"), vmem_limit_bytes=VMEM_LIMIT),
            name="stick_breaking_t",
        )(*args)
    return pl.pallas_call(
        functools.partial(_sb_kernel, QB=QB, KP=KP, n_past=n_past),
        grid=(bsz, t // QB),
        in_specs=in_specs,
        out_specs=pl.BlockSpec((1, QB, SB_W), lambda b, i: (b, i, 0)),
        out_shape=jax.ShapeDtypeStruct((bsz, t, SB_W), BF16),
        scratch_shapes=[pltpu.VMEM((QB, SB_DH), F32)] * (2 * SB_HEADS) + [pltpu.VMEM((QB, QB), F32)] * SB_HEADS
        + [pltpu.VMEM((QB, LANE), F32)] * SB_HEADS
        + ([pltpu.VMEM((QB, KP), F32)] * SB_HEADS + [pltpu.VMEM((QB, LANE), F32)] * SB_HEADS if n_past else []),
        compiler_params=pltpu.CompilerParams(
            dimension_semantics=("parallel", "arbitrary"), vmem_limit_bytes=VMEM_LIMIT),
        name="stick_breaking",
    )(*args)


def _outproj_kernel(*refs, final):
    if final:
        x_ref, a_ref, b_ref, c_ref, w_ref, fw_ref, o_ref = refs
    else:
        x_ref, a_ref, b_ref, c_ref, w_ref, o_ref = refs
    y = jnp.dot(a_ref[...], w_ref[0:HG_W, :], preferred_element_type=F32)
    y = y + jnp.dot(b_ref[...], w_ref[HG_W:HG_W + SSM_INNER, :], preferred_element_type=F32)
    y = y + jnp.dot(c_ref[...], w_ref[HG_W + SSM_INNER:, :], preferred_element_type=F32)
    x = x_ref[...] + y
    if final:
        ms = jnp.mean(x * x, axis=-1, keepdims=True)
        x = x * lax.rsqrt(ms + EPS) * fw_ref[...]
    o_ref[...] = x


def _out_proj(x2d, o_hg, y_ssm, o_sb, w_bf, final_w):
    m, d = x2d.shape
    tm = min(1024, m)
    final = final_w is not None

    def rows(w):
        return pl.BlockSpec((tm, w), lambda i: (i, 0))

    in_specs = [rows(d), rows(HG_W), rows(SSM_INNER), rows(SB_W),
                pl.BlockSpec(w_bf.shape, lambda i: (0, 0))]
    args = [x2d, o_hg, y_ssm, o_sb, w_bf]
    if final:
        in_specs.append(pl.BlockSpec((1, d), lambda i: (0, 0)))
        args.append(final_w.reshape(1, d))
    return pl.pallas_call(
        functools.partial(_outproj_kernel, final=final),
        grid=(m // tm,),
        in_specs=in_specs,
        out_specs=rows(d),
        out_shape=jax.ShapeDtypeStruct((m, d), F32),
        compiler_params=pltpu.CompilerParams(
            dimension_semantics=("parallel",), vmem_limit_bytes=VMEM_LIMIT),
        name="out_proj",
    )(*args)


def _reorder_w_in(w_in):
    hg_cols = 4 * HG_W
    ssm_cols = SSM_INNER + CONV_DIM + SSM_HEADS
    hg = w_in[..., :hg_cols]
    z_xbc = w_in[..., hg_cols:hg_cols + SSM_INNER + CONV_DIM]
    dt = w_in[..., hg_cols + SSM_INNER + CONV_DIM:hg_cols + ssm_cols]
    sb = w_in[..., hg_cols + ssm_cols:]
    sb_q = sb[..., :SB_W] * (LOG2E * SB_DH ** -0.5)
    sb_kv = sb[..., SB_W:3 * SB_W]
    sb_g = sb[..., 3 * SB_W:]
    z = z_xbc[..., :SSM_INNER]
    xbc = z_xbc[..., SSM_INNER:]
    used = w_in.shape[-1]
    pad = jnp.zeros(w_in.shape[:-1] + (NP - used,), w_in.dtype)
    return jnp.concatenate([sb_q, sb_kv, sb_g, xbc, dt, pad, z, hg], axis=-1).astype(BF16)


def _lower_bounds(lb_logits):
    p = jax.nn.softmax(lb_logits.astype(F32), axis=0)
    return jnp.clip(jnp.cumsum(p, axis=0) - p[0:1], 0.0, 1.0)


def _layer(x, states, kv_all, layer, depth, lb, norm_w, w_in_bf, conv_w, conv_b, dt_bias, a_log, d_skip,
           ssm_norm_w, hg_norm_w, w_out_bf, final_w):
    bsz, t, d = x.shape
    conv0, ssm0, hg0, k_past, v_past = states
    x2d = x.reshape(bsz * t, d)
    rest, qkv, k_all, v_all = _in_proj(x2d, norm_w, w_in_bf, layer, depth, *kv_all)
    rest = rest.reshape(bsz, t, NR)
    qkv = qkv.reshape(bsz, t, QKV_W)
    o_hg, hg_state = _hgrn(rest, lb, hg_norm_w, hg0)
    y_ssm, ssm_state, conv_state = _ssd(rest, conv_w, conv_b, dt_bias, a_log, d_skip, ssm_norm_w, conv0, ssm0)
    o_sb = _stick_breaking(rest, qkv, k_past, v_past, layer)
    x_new = _out_proj(x2d, o_hg.reshape(bsz * t, HG_W), y_ssm.reshape(bsz * t, SSM_INNER),
                      o_sb.reshape(bsz * t, SB_W), w_out_bf, final_w).reshape(bsz, t, d)
    return x_new, (hg_state, ssm_state, conv_state), (k_all, v_all)


def _run_group(x, layer_states, lbs, w_in_bf, w_out_bf, params, final_norm_w):
    norm_w, conv_w, conv_b, dt_bias, a_log, d_skip, ssm_norm_w, hg_norm_w = params
    depth = norm_w.shape[0]
    bsz, t, _ = x.shape
    outs = []
    kv_all = (None, None)
    for l in range(depth):
        x, st, kv_all = _layer(x, layer_states(l), kv_all, l, depth, lbs[l], norm_w[l], w_in_bf[l], conv_w[l],
                               conv_b[l], dt_bias[l], a_log[l], d_skip[l], ssm_norm_w[l], hg_norm_w[l],
                               w_out_bf[l], final_norm_w if l == depth - 1 else None)
        outs.append(st)
    kv = tuple(a.reshape(depth, bsz, t, SB_HEADS, SB_DH) for a in kv_all)
    return x, tuple(jnp.stack([st[i] for st in outs]) for i in range(3)) + kv


def kernel(x_prompt, x_sample, state_hgrn, state_ssm, state_conv, cache_k, cache_v, norm_w, w_in, lb_logits,
           conv_w, conv_b, dt_bias, a_log, d_skip, ssm_norm_w, hg_norm_w, w_out, final_norm_w):
    lbs = _lower_bounds(lb_logits)
    w_in_bf = _reorder_w_in(w_in)
    w_out_bf = w_out.astype(BF16)
    params = (norm_w, conv_w, conv_b, dt_bias, a_log, d_skip, ssm_norm_w, hg_norm_w)

    y_prompt, p_st = _run_group(x_prompt, lambda l: (None, None, None, None, None),
                                lbs, w_in_bf, w_out_bf, params, final_norm_w)
    k_cache = cache_k.reshape(cache_k.shape[:3] + (SB_W,))
    v_cache = cache_v.reshape(cache_v.shape[:3] + (SB_W,))
    y_sample, s_st = _run_group(x_sample,
                                lambda l: (state_conv[l], state_ssm[l], state_hgrn[l], k_cache, v_cache),
                                lbs, w_in_bf, w_out_bf, params, final_norm_w)
    return (y_prompt, y_sample) + p_st + s_st
```

```python
import functools

import jax
import jax.numpy as jnp
from jax import lax
from jax.experimental import pallas as pl
from jax.experimental.pallas import tpu as pltpu

F32 = jnp.float32
BF16 = jnp.bfloat16
EPS = 1e-6

LANE = 128
HG_HEADS = 4
HG_D = 128
SSM_HEADS = 16
SSM_P = 64
SSM_N = 128
SSM_GROUPS = 2
SSM_INNER = SSM_HEADS * SSM_P
CONV_W = 4
CONV_DIM = SSM_INNER + 2 * SSM_GROUPS * SSM_N
SB_HEADS = 4
SB_DH = 128
HG_W = HG_HEADS * HG_D
SB_W = SB_HEADS * SB_DH

PROJ_TN = 8 * LANE
QKV_TILES = 2
QKV_W = QKV_TILES * PROJ_TN
NB_XBC = 0
NB_DT = 12
NB_Z = 16
NB_HG_Q, NB_HG_F, NB_HG_I, NB_HG_G = 24, 28, 32, 36
NB_SB_G = 44
REST_TILES = 6
NR = REST_TILES * PROJ_TN
NP = 7 * PROJ_TN
LOG2E = 1.4426950408889634

VMEM_LIMIT = 56 * 1024 * 1024

SSD_PAD = 8


def _sigmoid(x):
    return 1.0 / (1.0 + jnp.exp(-x))


def _cumsum_rows_scan(x):
    n = x.shape[0]
    row = lax.broadcasted_iota(jnp.int32, x.shape, 0)
    sh = 1
    while sh < n:
        x = x + jnp.where(row >= sh, pltpu.roll(x, sh, axis=0), 0.0)
        sh *= 2
    return x


def _split3(x):
    hi = x.astype(BF16)
    r1 = x - hi.astype(F32)
    mid = r1.astype(BF16)
    lo = (r1 - mid.astype(F32)).astype(BF16)
    return hi, mid, lo


def _cumsum_rows(x):
    n = x.shape[0]
    tri = (lax.broadcasted_iota(jnp.int32, (n, n), 0) >= lax.broadcasted_iota(jnp.int32, (n, n), 1))
    tri = jnp.where(tri, 1.0, 0.0).astype(BF16)
    hi, mid, lo = _split3(x)
    out = jnp.dot(tri, lo, preferred_element_type=F32)
    out = out + jnp.dot(tri, mid, preferred_element_type=F32)
    return out + jnp.dot(tri, hi, preferred_element_type=F32)


def _transpose_rows(x):
    n = x.shape[0]
    if n < LANE:
        x = jnp.concatenate([x, jnp.zeros((LANE - n, LANE), x.dtype)], axis=0)
    return x.T[:, :n]


def _inproj_kernel(*refs, aliased):
    if aliased:
        x_ref, nw_ref, w_ref, _, _, rest_ref, qkv_ref, k_ref, v_ref, xn_ref = refs
    else:
        x_ref, nw_ref, w_ref, rest_ref, qkv_ref, k_ref, v_ref, xn_ref = refs
    j = pl.program_id(1)

    @pl.when(j == 0)
    def _():
        x = x_ref[...]
        ms = jnp.mean(x * x, axis=-1, keepdims=True)
        xn_ref[...] = (x * lax.rsqrt(ms + EPS) * nw_ref[...]).astype(BF16)

    acc = jnp.dot(xn_ref[...], w_ref[...], preferred_element_type=F32)

    @pl.when(j < QKV_TILES)
    def _():
        qkv_ref[...] = acc.astype(BF16)

    @pl.when(j == 0)
    def _():
        k_ref[0] = acc[:, SB_W:]

    @pl.when(j == 1)
    def _():
        v_ref[0] = acc[:, :SB_W]

    @pl.when(j >= 1)
    def _():
        rest_ref[...] = acc


def _in_proj(x2d, norm_w, w_bf, layer, depth, k_all, v_all):
    m, d = x2d.shape
    tm = min(1024, m)
    aliased = k_all is not None
    in_specs = [
        pl.BlockSpec((tm, d), lambda i, j: (i, 0)),
        pl.BlockSpec((1, d), lambda i, j: (0, 0)),
        pl.BlockSpec((d, PROJ_TN), lambda i, j: (0, j)),
    ]
    args = [x2d, norm_w.reshape(1, d), w_bf]
    if aliased:
        in_specs += [pl.BlockSpec(memory_space=pl.ANY), pl.BlockSpec(memory_space=pl.ANY)]
        args += [k_all, v_all]
    kv_spec = pl.BlockSpec((1, tm, SB_W), lambda i, j: (layer, i, 0))
    kv_shape = jax.ShapeDtypeStruct((depth, m, SB_W), F32)
    return pl.pallas_call(
        functools.partial(_inproj_kernel, aliased=aliased),
        grid=(m // tm, NP // PROJ_TN),
        in_specs=in_specs,
        out_specs=[pl.BlockSpec((tm, PROJ_TN), lambda i, j: (i, jnp.where(j < QKV_TILES, REST_TILES - 1, j - QKV_TILES))),
                   pl.BlockSpec((tm, PROJ_TN), lambda i, j: (i, jnp.minimum(j, QKV_TILES - 1))),
                   kv_spec, kv_spec],
        out_shape=[jax.ShapeDtypeStruct((m, NR), F32),
                   jax.ShapeDtypeStruct((m, QKV_W), BF16),
                   kv_shape, kv_shape],
        input_output_aliases={3: 2, 4: 3} if aliased else {},
        scratch_shapes=[pltpu.VMEM((tm, d), BF16)],
        compiler_params=pltpu.CompilerParams(
            dimension_semantics=("parallel", "arbitrary"), vmem_limit_bytes=VMEM_LIMIT),
        name="in_proj",
    )(*args)


def _hgrn_kernel(*refs, C, has_state):
    if has_state:
        q_ref, f_ref, i_ref, g_ref, lb_ref, nw_ref, s0_ref, o_ref, st_ref = refs[:9]
        scr = refs[9:]
    else:
        q_ref, f_ref, i_ref, g_ref, lb_ref, nw_ref, o_ref, st_ref = refs[:8]
        scr = refs[8:]
    sT_refs, b_refs = scr[:HG_HEADS], scr[HG_HEADS:]
    c = pl.program_id(1)

    @pl.when(c == 0)
    def _():
        for h in range(HG_HEADS):
            if has_state:
                sT_refs[h][...] = s0_ref[0, h].T
            else:
                sT_refs[h][...] = jnp.zeros((HG_D, HG_D), F32)

    for h in range(HG_HEADS):
        _hgrn_head(h, q_ref, f_ref, i_ref, g_ref, lb_ref, nw_ref, o_ref, sT_refs[h], b_refs[h], C)

    @pl.when(c == pl.num_programs(1) - 1)
    def _():
        for h in range(HG_HEADS):
            st_ref[0, h] = sT_refs[h][...].T


def _hgrn_head(head, q_ref, f_ref, i_ref, g_ref, lb_ref, nw_ref, o_ref, sT_ref, b_ref, C):
    cols = slice(head * HG_D, (head + 1) * HG_D)
    hq = q_ref[0, :, cols]
    hf = f_ref[0, :, cols]
    v = i_ref[0, :, cols]
    gate = g_ref[0, :, cols]
    lb = lb_ref[:, cols]

    q = hq * _sigmoid(hq)
    e = jnp.exp(-jnp.abs(hf))
    la = jnp.log(lb)
    lc = jnp.log1p(-lb) + (jnp.minimum(hf, 0.0) - jnp.log(1.0 + e))
    log_f = jnp.maximum(la, lc) + jnp.log(1.0 + jnp.exp(-jnp.abs(la - lc)))
    k = (1.0 - lb) * (jnp.where(hf >= 0.0, e, 1.0) / (1.0 + e))

    b = _cumsum_rows_scan(log_f)
    b_ref[...] = b
    row = lax.broadcasted_iota(jnp.int32, (C, HG_D), 0)

    small = [n for n in (8, 4, 2) if n <= C]
    rr = lax.broadcasted_iota(jnp.int32, (C, C), 0)
    cc = lax.broadcasted_iota(jnp.int32, (C, C), 1)
    sel = jnp.concatenate(
        [jnp.where(cc == ((rr & ~(n - 1)) + n // 2 - 1), 1.0, 0.0).astype(BF16) for n in small], axis=0)
    b3 = jnp.concatenate(_split3(b), axis=1)
    g3 = jnp.dot(sel, b3, preferred_element_type=F32)
    b_mid_small = {}
    for i, n in enumerate(small):
        g = g3[i * C:(i + 1) * C]
        b_mid_small[n] = (g[:, 2 * HG_D:] + g[:, HG_D:2 * HG_D]) + g[:, :HG_D]

    att = jnp.zeros((C, C), F32)
    n = C
    while n >= 2:
        h = n // 2
        if n >= 16:
            pieces = [jnp.broadcast_to(b_ref[blk * n + h - 1:blk * n + h, :], (n, HG_D)) for blk in range(C // n)]
            b_mid = pieces[0] if len(pieces) == 1 else jnp.concatenate(pieces, axis=0)
        else:
            b_mid = b_mid_small[n]
        second = (row & (n - 1)) >= h
        ex = jnp.exp(jnp.where(second, b - b_mid, b_mid - b))
        q_n = jnp.where(second, q * ex, 0.0).astype(BF16)
        k_n = jnp.where(second, 0.0, k * ex).astype(BF16)
        a_n = lax.dot_general(q_n, k_n, (((1,), (1,)), ((), ())), preferred_element_type=F32)
        if n < C:
            shift = n.bit_length() - 1
            a_n = jnp.where((rr >> shift) == (cc >> shift), a_n, 0.0)
        att = att + a_n
        n = h

    ones = jnp.ones((HG_D, HG_D), BF16)
    o = jnp.dot((q * k).astype(BF16), ones, preferred_element_type=F32) * v

    s_t = sT_ref[...]
    v_bf = v.astype(BF16)
    o = o + lax.dot_general((q * jnp.exp(b)).astype(BF16), s_t.astype(BF16),
                            (((1,), (1,)), ((), ())), preferred_element_type=F32)
    o = o + jnp.dot(att.astype(BF16), v_bf, preferred_element_type=F32)

    b_last = b_ref[C - 1:C, :]
    k_dec = (k * jnp.exp(b_last - b)).astype(BF16)
    upd = lax.dot_general(v_bf, k_dec, (((0,), (0,)), ((), ())), preferred_element_type=F32)
    sT_ref[...] = s_t * jnp.exp(b_last) + upd

    ms = jnp.mean(o * o, axis=-1, keepdims=True)
    o = o * lax.rsqrt(ms + EPS) * nw_ref[...]
    o_ref[0, :, cols] = (o * (gate * _sigmoid(gate))).astype(o_ref.dtype)


def _hgrn(proj, lb, nw, s0):
    bsz, t, _ = proj.shape
    C = min(128, t)
    has_state = s0 is not None

    def col(nb):
        return pl.BlockSpec((1, C, HG_W), lambda b, c: (b, c, nb // HG_HEADS))

    in_specs = [col(NB_HG_Q), col(NB_HG_F), col(NB_HG_I), col(NB_HG_G),
                pl.BlockSpec((1, HG_W), lambda b, c: (0, 0)),
                pl.BlockSpec((1, HG_D), lambda b, c: (0, 0))]
    args = [proj, proj, proj, proj, lb.reshape(1, HG_W), nw.reshape(1, HG_D)]
    if has_state:
        in_specs.append(pl.BlockSpec((1, HG_HEADS, HG_D, HG_D), lambda b, c: (b, 0, 0, 0)))
        args.append(s0)
    return pl.pallas_call(
        functools.partial(_hgrn_kernel, C=C, has_state=has_state),
        grid=(bsz, t // C),
        in_specs=in_specs,
        out_specs=[pl.BlockSpec((1, C, HG_W), lambda b, c: (b, c, 0)),
                   pl.BlockSpec((1, HG_HEADS, HG_D, HG_D), lambda b, c: (b, 0, 0, 0))],
        out_shape=[jax.ShapeDtypeStruct((bsz, t, HG_W), BF16),
                   jax.ShapeDtypeStruct((bsz, HG_HEADS, HG_D, HG_D), F32)],
        scratch_shapes=[pltpu.VMEM((HG_D, HG_D), F32)] * HG_HEADS + [pltpu.VMEM((C, HG_D), F32)] * HG_HEADS,
        compiler_params=pltpu.CompilerParams(
            dimension_semantics=("parallel", "arbitrary"), vmem_limit_bytes=VMEM_LIMIT),
        name="hgrn2_scan",
    )(*args)


def _ssd_kernel(*refs, L, has_state):
    if has_state:
        (z_ref, xbc_ref, dt_ref, cw_ref, cb_ref, dtb_ref, al_ref, dsk_ref, nw_ref, conv0_ref, s0_ref,
         y_ref, st_ref, cst_ref, xp_ref, s_ref, ys_ref) = refs
    else:
        (z_ref, xbc_ref, dt_ref, cw_ref, cb_ref, dtb_ref, al_ref, dsk_ref, nw_ref,
         y_ref, st_ref, cst_ref, xp_ref, s_ref, ys_ref) = refs
    c = pl.program_id(1)

    @pl.when(c == 0)
    def _():
        xp_ref[0:SSD_PAD, :] = jnp.zeros((SSD_PAD, CONV_DIM), F32)
        if has_state:
            xp_ref[SSD_PAD - (CONV_W - 1):SSD_PAD, :] = conv0_ref[0]
            s_ref[...] = s0_ref[0]
        else:
            s_ref[...] = jnp.zeros((SSM_HEADS, SSM_P, SSM_N), F32)

    xp_ref[SSD_PAD:SSD_PAD + L, :] = xbc_ref[0]
    cw = cw_ref[...]
    conv = cb_ref[...]
    for j in range(CONV_W):
        lo = SSD_PAD - (CONV_W - 1) + j
        conv = conv + cw[j:j + 1, :] * xp_ref[lo:lo + L, :]
    cst_ref[0] = xp_ref[L + SSD_PAD - (CONV_W - 1):L + SSD_PAD, :]
    xp_ref[0:SSD_PAD, :] = xp_ref[L:L + SSD_PAD, :]
    act = conv * _sigmoid(conv)
    xs = act[:, :SSM_INNER]

    dt_raw = dt_ref[0] + dtb_ref[...]
    dt = jnp.maximum(dt_raw, 0.0) + jnp.log(1.0 + jnp.exp(-jnp.abs(dt_raw)))
    a = -jnp.exp(al_ref[...])
    cs = _cumsum_rows(dt * a)
    cs_t = _transpose_rows(cs)
    dt_t = _transpose_rows(dt)

    tril = lax.broadcasted_iota(jnp.int32, (L, L), 0) >= lax.broadcasted_iota(jnp.int32, (L, L), 1)
    hpg = SSM_HEADS // SSM_GROUPS
    for g in range(SSM_GROUPS):
        b_g = act[:, SSM_INNER + g * SSM_N:SSM_INNER + (g + 1) * SSM_N].astype(BF16)
        c_g = act[:, SSM_INNER + (SSM_GROUPS + g) * SSM_N:SSM_INNER + (SSM_GROUPS + g + 1) * SSM_N].astype(BF16)
        cb = lax.dot_general(c_g, b_g, (((1,), (1,)), ((), ())), preferred_element_type=F32)
        for hh in range(hpg):
            h = g * hpg + hh
            cs_col = cs[:, h:h + 1]
            cs_row = cs_t[h:h + 1, :]
            decay = jnp.exp(jnp.where(tril, cs_col - cs_row, -jnp.inf))
            w = (cb * decay * dt_t[h:h + 1, :]).astype(BF16)
            x_h = xs[:, h * SSM_P:(h + 1) * SSM_P]
            s_h = s_ref[h]
            y_h = jnp.dot(w, x_h.astype(BF16), preferred_element_type=F32)
            y_h = y_h + jnp.exp(cs_col) * lax.dot_general(
                c_g, s_h.astype(BF16), (((1,), (1,)), ((), ())), preferred_element_type=F32)
            ys_ref[:, h * SSM_P:(h + 1) * SSM_P] = y_h
            cs_last = cs_t[h:h + 1, L - 1:L]
            wdec = jnp.exp(cs_last - cs_col) * dt[:, h:h + 1]
            upd = lax.dot_general((x_h * wdec).astype(BF16), b_g, (((0,), (0,)), ((), ())),
                                  preferred_element_type=F32)
            s_ref[h] = jnp.exp(cs_last) * s_h + upd

    z = z_ref[0]
    y = (ys_ref[...] + dsk_ref[...] * xs) * (z * _sigmoid(z))
    gsz = SSM_INNER // SSM_GROUPS
    nw = nw_ref[...]
    for g in range(SSM_GROUPS):
        yg = y[:, g * gsz:(g + 1) * gsz]
        ms = jnp.mean(yg * yg, axis=-1, keepdims=True)
        y_ref[0, :, g * gsz:(g + 1) * gsz] = (yg * lax.rsqrt(ms + EPS) * nw[:, g * gsz:(g + 1) * gsz]).astype(y_ref.dtype)

    @pl.when(c == pl.num_programs(1) - 1)
    def _():
        st_ref[0] = s_ref[...]


def _pad_lanes(v):
    return jnp.pad(v.astype(F32), (0, LANE - v.shape[0])).reshape(1, LANE)


def _ssd(proj, conv_w, conv_b, dt_bias, a_log, d_skip, ssm_norm_w, conv0, s0):
    bsz, t, _ = proj.shape
    L = min(256, t)
    has_state = s0 is not None

    def const(shape):
        return pl.BlockSpec(shape, lambda b, c: (0,) * len(shape))

    in_specs = [pl.BlockSpec((1, L, SSM_INNER), lambda b, c: (b, c, NB_Z * LANE // SSM_INNER)),
                pl.BlockSpec((1, L, CONV_DIM), lambda b, c: (b, c, NB_XBC * LANE // CONV_DIM)),
                pl.BlockSpec((1, L, LANE), lambda b, c: (b, c, NB_DT)),
                const((CONV_W, CONV_DIM)), const((1, CONV_DIM)), const((1, LANE)), const((1, LANE)),
                const((1, SSM_INNER)), const((1, SSM_INNER))]
    args = [proj, proj, proj, conv_w, conv_b.reshape(1, CONV_DIM), _pad_lanes(dt_bias), _pad_lanes(a_log),
            jnp.repeat(d_skip.astype(F32), SSM_P).reshape(1, SSM_INNER), ssm_norm_w.reshape(1, SSM_INNER)]
    if has_state:
        in_specs += [pl.BlockSpec((1, CONV_W - 1, CONV_DIM), lambda b, c: (b, 0, 0)),
                     pl.BlockSpec((1, SSM_HEADS, SSM_P, SSM_N), lambda b, c: (b, 0, 0, 0))]
        args += [conv0, s0]
    return pl.pallas_call(
        functools.partial(_ssd_kernel, L=L, has_state=has_state),
        grid=(bsz, t // L),
        in_specs=in_specs,
        out_specs=[pl.BlockSpec((1, L, SSM_INNER), lambda b, c: (b, c, 0)),
                   pl.BlockSpec((1, SSM_HEADS, SSM_P, SSM_N), lambda b, c: (b, 0, 0, 0)),
                   pl.BlockSpec((1, CONV_W - 1, CONV_DIM), lambda b, c: (b, 0, 0))],
        out_shape=[jax.ShapeDtypeStruct((bsz, t, SSM_INNER), BF16),
                   jax.ShapeDtypeStruct((bsz, SSM_HEADS, SSM_P, SSM_N), F32),
                   jax.ShapeDtypeStruct((bsz, CONV_W - 1, CONV_DIM), F32)],
        scratch_shapes=[pltpu.VMEM((L + SSD_PAD, CONV_DIM), F32),
                        pltpu.VMEM((SSM_HEADS, SSM_P, SSM_N), F32),
                        pltpu.VMEM((L, SSM_INNER), F32)],
        compiler_params=pltpu.CompilerParams(
            dimension_semantics=("parallel", "arbitrary"), vmem_limit_bytes=VMEM_LIMIT),
        name="conv_ssd_scan",
    )(*args)


def _neg_tri_ext(kb):
    kk = lax.broadcasted_iota(jnp.int32, (kb, kb + LANE), 0)
    jj = lax.broadcasted_iota(jnp.int32, (kb, kb + LANE), 1)
    return jnp.where((kk >= jj) | (jj >= kb), -1.0, 0.0).astype(BF16)


def _sb_kernel(*refs, QB, KP, n_past):
    if n_past:
        q_ref, k_ref, v_ref, g_ref, kp_ref, vp_ref, o_ref = refs[:7]
        scr = refs[7:]
    else:
        q_ref, k_ref, v_ref, g_ref, o_ref = refs[:5]
        scr = refs[5:]
    nh = SB_HEADS
    acc_refs, r_refs, xn_ref, tn_ref = scr[0:nh], scr[nh:2 * nh], scr[2 * nh:3 * nh], scr[3 * nh:4 * nh]
    if n_past:
        xp_ref, tp_ref = scr[4 * nh:5 * nh], scr[5 * nh:6 * nh]
    qi = pl.program_id(1)
    nt = (((1,), (1,)), ((), ()))

    def lanes(h):
        return slice(h * SB_DH, (h + 1) * SB_DH)

    def qk(h, k_t):
        return lax.dot_general(q_ref[0, :, lanes(h)], k_t, nt, preferred_element_type=F32)

    def scores(h, z, ntri, x_ref, t_ref, mask=None):
        kb = z.shape[1]
        sp = jnp.maximum(z, 0.0) + jnp.log2(1.0 + jnp.exp2(-jnp.abs(z)))
        if mask is not None:
            sp = jnp.where(mask, sp, 0.0)
        wt = jnp.dot(sp.astype(BF16), ntri, preferred_element_type=F32)
        x = z + wt[:, :kb]
        if mask is not None:
            x = jnp.where(mask, x, -jnp.inf)
        x_ref[h][...] = x
        t_ref[h][...] = wt[:, kb:]

    def accumulate(h, v_t, x_ref, t_ref):
        kb = v_t.shape[0]
        r = r_refs[h][...]
        w = jnp.exp2(x_ref[h][...] + jnp.tile(r, (1, max(kb // LANE, 1)))[:, :kb])
        acc_refs[h][...] += jnp.dot(w.astype(BF16), v_t, preferred_element_type=F32)
        r_refs[h][...] = r + t_ref[h][...]

    for h in range(SB_HEADS):
        acc_refs[h][...] = jnp.zeros_like(acc_refs[h])
        r_refs[h][...] = jnp.zeros_like(r_refs[h])

    start = pl.multiple_of(qi * QB, QB)
    mask = lax.broadcasted_iota(jnp.int32, (QB, QB), 1) < lax.broadcasted_iota(jnp.int32, (QB, QB), 0)
    ntri_q = _neg_tri_ext(QB)
    for h in range(SB_HEADS):
        scores(h, qk(h, k_ref[0, pl.ds(start, QB), lanes(h)]), ntri_q, xn_ref, tn_ref, mask=mask)

    if QB % LANE == 0:
        def new_body(jj, carry):
            s = pl.multiple_of((qi - 1 - jj) * QB, QB)
            zs = [qk(h, k_ref[0, pl.ds(s, QB), lanes(h)]) for h in range(SB_HEADS)]
            for h in range(SB_HEADS):
                accumulate(h, v_ref[0, pl.ds(s + QB, QB), lanes(h)], xn_ref, tn_ref)
            for h in range(SB_HEADS):
                scores(h, zs[h], ntri_q, xn_ref, tn_ref)
            return carry

        lax.fori_loop(0, qi, new_body, 0)

    if n_past:
        ntri_p = _neg_tri_ext(KP)
        last = (n_past - 1) * KP
        for h in range(SB_HEADS):
            accumulate(h, v_ref[0, 0:QB, lanes(h)], xn_ref, tn_ref)
            scores(h, qk(h, kp_ref[0, last:last + KP, lanes(h)].astype(BF16)), ntri_p, xp_ref, tp_ref)

        def past_body(jj, carry):
            s = pl.multiple_of((n_past - 1 - jj) * KP, KP)
            zs = [qk(h, kp_ref[0, pl.ds(s, KP), lanes(h)].astype(BF16)) for h in range(SB_HEADS)]
            for h in range(SB_HEADS):
                accumulate(h, vp_ref[0, pl.ds(s + KP, KP), lanes(h)].astype(BF16), xp_ref, tp_ref)
            for h in range(SB_HEADS):
                scores(h, zs[h], ntri_p, xp_ref, tp_ref)
            return carry

        lax.fori_loop(1, n_past, past_body, 0)
        for h in range(SB_HEADS):
            accumulate(h, vp_ref[0, 0:KP, lanes(h)].astype(BF16), xp_ref, tp_ref)
    else:
        for h in range(SB_HEADS):
            accumulate(h, v_ref[0, 0:QB, lanes(h)], xn_ref, tn_ref)

    gate = g_ref[0]
    for h in range(SB_HEADS):
        gh = gate[:, lanes(h)]
        o_ref[0, :, lanes(h)] = (acc_refs[h][...] * (gh * _sigmoid(gh))).astype(o_ref.dtype)


def _neg_tri_ext_t(kb):
    jj = lax.broadcasted_iota(jnp.int32, (kb + 16, kb), 0)
    kk = lax.broadcasted_iota(jnp.int32, (kb + 16, kb), 1)
    return jnp.where((kk >= jj) | (jj >= kb), -1.0, 0.0).astype(BF16)


def _sbt_kernel(q_ref, k_ref, v_ref, g_ref, o_ref, *scr, QB):
    nh = SB_HEADS
    acc_refs, r_refs, x_refs, t_refs = scr[0:nh], scr[nh:2 * nh], scr[2 * nh:3 * nh], scr[3 * nh:4 * nh]
    qi = pl.program_id(1)
    nt = (((1,), (1,)), ((), ()))
    tn = (((0,), (0,)), ((), ()))
    ntri = _neg_tri_ext_t(QB)

    def lanes(h):
        return slice(h * SB_DH, (h + 1) * SB_DH)

    def qk(h, k_t):
        return lax.dot_general(k_t, q_ref[0, :, lanes(h)], nt, preferred_element_type=F32)

    def scores(h, z, mask=None):
        sp = jnp.maximum(z, 0.0) + jnp.log2(1.0 + jnp.exp2(-jnp.abs(z)))
        if mask is not None:
            sp = jnp.where(mask, sp, 0.0)
        wt = jnp.dot(ntri, sp.astype(BF16), preferred_element_type=F32)
        x = z + wt[:QB]
        if mask is not None:
            x = jnp.where(mask, x, -jnp.inf)
        x_refs[h][...] = x
        t_refs[h][...] = wt[QB:QB + 8]

    def accumulate(h, v_t):
        r = r_refs[h][...]
        w = jnp.exp2(x_refs[h][...] + r[0:1, :])
        acc_refs[h][...] += lax.dot_general(v_t, w.astype(BF16), tn, preferred_element_type=F32)
        r_refs[h][...] = r + t_refs[h][...]

    for h in range(nh):
        acc_refs[h][...] = jnp.zeros_like(acc_refs[h])
        r_refs[h][...] = jnp.zeros_like(r_refs[h])

    start = pl.multiple_of(qi * QB, QB)
    mask = lax.broadcasted_iota(jnp.int32, (QB, QB), 0) < lax.broadcasted_iota(jnp.int32, (QB, QB), 1)
    for h in range(nh):
        scores(h, qk(h, k_ref[0, pl.ds(start, QB), lanes(h)]), mask=mask)

    def body(jj, carry):
        s = pl.multiple_of((qi - 1 - jj) * QB, QB)
        zs = [qk(h, k_ref[0, pl.ds(s, QB), lanes(h)]) for h in range(nh)]
        for h in range(nh):
            accumulate(h, v_ref[0, pl.ds(s + QB, QB), lanes(h)])
        for h in range(nh):
            scores(h, zs[h])
        return carry

    lax.fori_loop(0, qi, body, 0)
    gate = g_ref[0]
    for h in range(nh):
        accumulate(h, v_ref[0, 0:QB, lanes(h)])
        gh = gate[:, lanes(h)]
        o_ref[0, :, lanes(h)] = (acc_refs[h][...].T * (gh * _sigmoid(gh))).astype(o_ref.dtype)


def _stick_breaking(rest, qkv, k_past, v_past, layer):
    bsz, t, _ = rest.shape
    QB = min(256, t)
    KP = 256
    n_past = 0 if k_past is None else k_past.shape[2] // KP
    assert t % QB == 0 and (t == QB or QB % LANE == 0)

    in_specs = [pl.BlockSpec((1, QB, SB_W), lambda b, i: (b, i, 0)),
                pl.BlockSpec((1, t, SB_W), lambda b, i: (b, 0, 1)),
                pl.BlockSpec((1, t, SB_W), lambda b, i: (b, 0, 2)),
                pl.BlockSpec((1, QB, SB_W), lambda b, i: (b, i, NB_SB_G * LANE // SB_W))]
    args = [qkv, qkv, qkv, rest]
    if n_past:
        assert k_past.shape[2] % KP == 0
        past = pl.BlockSpec((None, 1, k_past.shape[2], SB_W), lambda b, i: (layer, b, 0, 0))
        in_specs += [past, past]
        args += [k_past, v_past]
    if not n_past and QB % LANE == 0:
        return pl.pallas_call(
            functools.partial(_sbt_kernel, QB=QB),
            grid=(bsz, t // QB),
            in_specs=in_specs,
            out_specs=pl.BlockSpec((1, QB, SB_W), lambda b, i: (b, i, 0)),
            out_shape=jax.ShapeDtypeStruct((bsz, t, SB_W), BF16),
            scratch_shapes=[pltpu.VMEM((SB_DH, QB), F32)] * SB_HEADS + [pltpu.VMEM((8, QB), F32)] * SB_HEADS
            + [pltpu.VMEM((QB, QB), F32)] * SB_HEADS + [pltpu.VMEM((8, QB), F32)] * SB_HEADS,
            compiler_params=pltpu.CompilerParams(
                dimension_semantics=("parallel", "arbitrary"), vmem_limit_bytes=VMEM_LIMIT),
            name="stick_breaking_t",
        )(*args)
    return pl.pallas_call(
        functools.partial(_sb_kernel, QB=QB, KP=KP, n_past=n_past),
        grid=(bsz, t // QB),
        in_specs=in_specs,
        out_specs=pl.BlockSpec((1, QB, SB_W), lambda b, i: (b, i, 0)),
        out_shape=jax.ShapeDtypeStruct((bsz, t, SB_W), BF16),
        scratch_shapes=[pltpu.VMEM((QB, SB_DH), F32)] * (2 * SB_HEADS) + [pltpu.VMEM((QB, QB), F32)] * SB_HEADS
        + [pltpu.VMEM((QB, LANE), F32)] * SB_HEADS
        + ([pltpu.VMEM((QB, KP), F32)] * SB_HEADS + [pltpu.VMEM((QB, LANE), F32)] * SB_HEADS if n_past else []),
        compiler_params=pltpu.CompilerParams(
            dimension_semantics=("parallel", "arbitrary"), vmem_limit_bytes=VMEM_LIMIT),
        name="stick_breaking",
    )(*args)


def _outproj_kernel(*refs, final):
    if final:
        x_ref, a_ref, b_ref, c_ref, w_ref, fw_ref, o_ref = refs
    else:
        x_ref, a_ref, b_ref, c_ref, w_ref, o_ref = refs
    y = jnp.dot(a_ref[...], w_ref[0:HG_W, :], preferred_element_type=F32)
    y = y + jnp.dot(b_ref[...], w_ref[HG_W:HG_W + SSM_INNER, :], preferred_element_type=F32)
    y = y + jnp.dot(c_ref[...], w_ref[HG_W + SSM_INNER:, :], preferred_element_type=F32)
    x = x_ref[...] + y
    if final:
        ms = jnp.mean(x * x, axis=-1, keepdims=True)
        x = x * lax.rsqrt(ms + EPS) * fw_ref[...]
    o_ref[...] = x


def _out_proj(x2d, o_hg, y_ssm, o_sb, w_bf, final_w):
    m, d = x2d.shape
    tm = min(1024, m)
    final = final_w is not None

    def rows(w):
        return pl.BlockSpec((tm, w), lambda i: (i, 0))

    in_specs = [rows(d), rows(HG_W), rows(SSM_INNER), rows(SB_W),
                pl.BlockSpec(w_bf.shape, lambda i: (0, 0))]
    args = [x2d, o_hg, y_ssm, o_sb, w_bf]
    if final:
        in_specs.append(pl.BlockSpec((1, d), lambda i: (0, 0)))
        args.append(final_w.reshape(1, d))
    return pl.pallas_call(
        functools.partial(_outproj_kernel, final=final),
        grid=(m // tm,),
        in_specs=in_specs,
        out_specs=rows(d),
        out_shape=jax.ShapeDtypeStruct((m, d), F32),
        compiler_params=pltpu.CompilerParams(
            dimension_semantics=("parallel",), vmem_limit_bytes=VMEM_LIMIT),
        name="out_proj",
    )(*args)


def _reorder_w_in(w_in):
    hg_cols = 4 * HG_W
    ssm_cols = SSM_INNER + CONV_DIM + SSM_HEADS
    hg = w_in[..., :hg_cols]
    z_xbc = w_in[..., hg_cols:hg_cols + SSM_INNER + CONV_DIM]
    dt = w_in[..., hg_cols + SSM_INNER + CONV_DIM:hg_cols + ssm_cols]
    sb = w_in[..., hg_cols + ssm_cols:]
    sb_q = sb[..., :SB_W] * (LOG2E * SB_DH ** -0.5)
    sb_kv = sb[..., SB_W:3 * SB_W]
    sb_g = sb[..., 3 * SB_W:]
    z = z_xbc[..., :SSM_INNER]
    xbc = z_xbc[..., SSM_INNER:]
    used = w_in.shape[-1]
    pad = jnp.zeros(w_in.shape[:-1] + (NP - used,), w_in.dtype)
    return jnp.concatenate([sb_q, sb_kv, sb_g, xbc, dt, pad, z, hg], axis=-1).astype(BF16)


def _lower_bounds(lb_logits):
    p = jax.nn.softmax(lb_logits.astype(F32), axis=0)
    return jnp.clip(jnp.cumsum(p, axis=0) - p[0:1], 0.0, 1.0)


def _layer(x, states, kv_all, layer, depth, lb, norm_w, w_in_bf, conv_w, conv_b, dt_bias, a_log, d_skip,
           ssm_norm_w, hg_norm_w, w_out_bf, final_w):
    bsz, t, d = x.shape
    conv0, ssm0, hg0, k_past, v_past = states
    x2d = x.reshape(bsz * t, d)
    rest, qkv, k_all, v_all = _in_proj(x2d, norm_w, w_in_bf, layer, depth, *kv_all)
    rest = rest.reshape(bsz, t, NR)
    qkv = qkv.reshape(bsz, t, QKV_W)
    o_hg, hg_state = _hgrn(rest, lb, hg_norm_w, hg0)
    y_ssm, ssm_state, conv_state = _ssd(rest, conv_w, conv_b, dt_bias, a_log, d_skip, ssm_norm_w, conv0, ssm0)
    o_sb = _stick_breaking(rest, qkv, k_past, v_past, layer)
    x_new = _out_proj(x2d, o_hg.reshape(bsz * t, HG_W), y_ssm.reshape(bsz * t, SSM_INNER),
                      o_sb.reshape(bsz * t, SB_W), w_out_bf, final_w).reshape(bsz, t, d)
    return x_new, (hg_state, ssm_state, conv_state), (k_all, v_all)


def _run_group(x, layer_states, lbs, w_in_bf, w_out_bf, params, final_norm_w):
    norm_w, conv_w, conv_b, dt_bias, a_log, d_skip, ssm_norm_w, hg_norm_w = params
    depth = norm_w.shape[0]
    bsz, t, _ = x.shape
    outs = []
    kv_all = (None, None)
    for l in range(depth):
        x, st, kv_all = _layer(x, layer_states(l), kv_all, l, depth, lbs[l], norm_w[l], w_in_bf[l], conv_w[l],
                               conv_b[l], dt_bias[l], a_log[l], d_skip[l], ssm_norm_w[l], hg_norm_w[l],
                               w_out_bf[l], final_norm_w if l == depth - 1 else None)
        outs.append(st)
    kv = tuple(a.reshape(depth, bsz, t, SB_HEADS, SB_DH) for a in kv_all)
    return x, tuple(jnp.stack([st[i] for st in outs]) for i in range(3)) + kv


def kernel(x_prompt, x_sample, state_hgrn, state_ssm, state_conv, cache_k, cache_v, norm_w, w_in, lb_logits,
           conv_w, conv_b, dt_bias, a_log, d_skip, ssm_norm_w, hg_norm_w, w_out, final_norm_w):
    lbs = _lower_bounds(lb_logits)
    w_in_bf = _reorder_w_in(w_in)
    w_out_bf = w_out.astype(BF16)
    params = (norm_w, conv_w, conv_b, dt_bias, a_log, d_skip, ssm_norm_w, hg_norm_w)

    y_prompt, p_st = _run_group(x_prompt, lambda l: (None, None, None, None, None),
                                lbs, w_in_bf, w_out_bf, params, final_norm_w)
    k_cache = cache_k.reshape(cache_k.shape[:3] + (SB_W,))
    v_cache = cache_v.reshape(cache_v.shape[:3] + (SB_W,))
    y_sample, s_st = _run_group(x_sample,
                                lambda l: (state_conv[l], state_ssm[l], state_hgrn[l], k_cache, v_cache),
                                lbs, w_in_bf, w_out_bf, params, final_norm_w)
    return (y_prompt, y_sample) + p_st + s_st
```

```python
import functools

import jax
import jax.numpy as jnp
from jax import lax
from jax.experimental import pallas as pl
from jax.experimental.pallas import tpu as pltpu

F32 = jnp.float32
BF16 = jnp.bfloat16
EPS = 1e-6

LANE = 128
HG_HEADS = 4
HG_D = 128
SSM_HEADS = 16
SSM_P = 64
SSM_N = 128
SSM_GROUPS = 2
SSM_INNER = SSM_HEADS * SSM_P
CONV_W = 4
CONV_DIM = SSM_INNER + 2 * SSM_GROUPS * SSM_N
SB_HEADS = 4
SB_DH = 128
HG_W = HG_HEADS * HG_D
SB_W = SB_HEADS * SB_DH

PROJ_TN = 8 * LANE
QKV_TILES = 2
QKV_W = QKV_TILES * PROJ_TN
NB_XBC = 0
NB_DT = 12
NB_Z = 16
NB_HG_Q, NB_HG_F, NB_HG_I, NB_HG_G = 24, 28, 32, 36
NB_SB_G = 44
REST_TILES = 6
NR = REST_TILES * PROJ_TN
NP = 7 * PROJ_TN
LOG2E = 1.4426950408889634

VMEM_LIMIT = 56 * 1024 * 1024

SSD_PAD = 8


def _sigmoid(x):
    return 1.0 / (1.0 + jnp.exp(-x))


def _cumsum_rows_scan(x):
    n = x.shape[0]
    row = lax.broadcasted_iota(jnp.int32, x.shape, 0)
    sh = 1
    while sh < n:
        x = x + jnp.where(row >= sh, pltpu.roll(x, sh, axis=0), 0.0)
        sh *= 2
    return x


def _split3(x):
    hi = x.astype(BF16)
    r1 = x - hi.astype(F32)
    mid = r1.astype(BF16)
    lo = (r1 - mid.astype(F32)).astype(BF16)
    return hi, mid, lo


def _cumsum_rows(x):
    n = x.shape[0]
    tri = (lax.broadcasted_iota(jnp.int32, (n, n), 0) >= lax.broadcasted_iota(jnp.int32, (n, n), 1))
    tri = jnp.where(tri, 1.0, 0.0).astype(BF16)
    hi, mid, lo = _split3(x)
    out = jnp.dot(tri, lo, preferred_element_type=F32)
    out = out + jnp.dot(tri, mid, preferred_element_type=F32)
    return out + jnp.dot(tri, hi, preferred_element_type=F32)


def _transpose_rows(x):
    n = x.shape[0]
    if n < LANE:
        x = jnp.concatenate([x, jnp.zeros((LANE - n, LANE), x.dtype)], axis=0)
    return x.T[:, :n]


def _inproj_kernel(*refs, aliased):
    if aliased:
        x_ref, nw_ref, w_ref, _, _, rest_ref, qkv_ref, k_ref, v_ref, xn_ref = refs
    else:
        x_ref, nw_ref, w_ref, rest_ref, qkv_ref, k_ref, v_ref, xn_ref = refs
    j = pl.program_id(1)

    @pl.when(j == 0)
    def _():
        x = x_ref[...]
        ms = jnp.mean(x * x, axis=-1, keepdims=True)
        xn_ref[...] = (x * lax.rsqrt(ms + EPS) * nw_ref[...]).astype(BF16)

    acc = jnp.dot(xn_ref[...], w_ref[...], preferred_element_type=F32)

    @pl.when(j < QKV_TILES)
    def _():
        qkv_ref[...] = acc.astype(BF16)

    @pl.when(j == 0)
    def _():
        k_ref[0] = acc[:, SB_W:]

    @pl.when(j == 1)
    def _():
        v_ref[0] = acc[:, :SB_W]

    @pl.when(j >= 1)
    def _():
        rest_ref[...] = acc


def _in_proj(x2d, norm_w, w_bf, layer, depth, k_all, v_all):
    m, d = x2d.shape
    tm = min(1024, m)
    aliased = k_all is not None
    in_specs = [
        pl.BlockSpec((tm, d), lambda i, j: (i, 0)),
        pl.BlockSpec((1, d), lambda i, j: (0, 0)),
        pl.BlockSpec((d, PROJ_TN), lambda i, j: (0, j)),
    ]
    args = [x2d, norm_w.reshape(1, d), w_bf]
    if aliased:
        in_specs += [pl.BlockSpec(memory_space=pl.ANY), pl.BlockSpec(memory_space=pl.ANY)]
        args += [k_all, v_all]
    kv_spec = pl.BlockSpec((1, tm, SB_W), lambda i, j: (layer, i, 0))
    kv_shape = jax.ShapeDtypeStruct((depth, m, SB_W), F32)
    return pl.pallas_call(
        functools.partial(_inproj_kernel, aliased=aliased),
        grid=(m // tm, NP // PROJ_TN),
        in_specs=in_specs,
        out_specs=[pl.BlockSpec((tm, PROJ_TN), lambda i, j: (i, jnp.where(j < QKV_TILES, REST_TILES - 1, j - QKV_TILES))),
                   pl.BlockSpec((tm, PROJ_TN), lambda i, j: (i, jnp.minimum(j, QKV_TILES - 1))),
                   kv_spec, kv_spec],
        out_shape=[jax.ShapeDtypeStruct((m, NR), F32),
                   jax.ShapeDtypeStruct((m, QKV_W), BF16),
                   kv_shape, kv_shape],
        input_output_aliases={3: 2, 4: 3} if aliased else {},
        scratch_shapes=[pltpu.VMEM((tm, d), BF16)],
        compiler_params=pltpu.CompilerParams(
            dimension_semantics=("parallel", "arbitrary"), vmem_limit_bytes=VMEM_LIMIT),
        name="in_proj",
    )(*args)


def _hgrn_kernel(*refs, C, has_state):
    if has_state:
        q_ref, f_ref, i_ref, g_ref, lb_ref, nw_ref, s0_ref, o_ref, st_ref = refs[:9]
        scr = refs[9:]
    else:
        q_ref, f_ref, i_ref, g_ref, lb_ref, nw_ref, o_ref, st_ref = refs[:8]
        scr = refs[8:]
    sT_refs, b_refs = scr[:HG_HEADS], scr[HG_HEADS:]
    c = pl.program_id(1)

    @pl.when(c == 0)
    def _():
        for h in range(HG_HEADS):
            if has_state:
                sT_refs[h][...] = s0_ref[0, h].T
            else:
                sT_refs[h][...] = jnp.zeros((HG_D, HG_D), F32)

    for h in range(HG_HEADS):
        _hgrn_head(h, q_ref, f_ref, i_ref, g_ref, lb_ref, nw_ref, o_ref, sT_refs[h], b_refs[h], C)

    @pl.when(c == pl.num_programs(1) - 1)
    def _():
        for h in range(HG_HEADS):
            st_ref[0, h] = sT_refs[h][...].T


def _hgrn_head(head, q_ref, f_ref, i_ref, g_ref, lb_ref, nw_ref, o_ref, sT_ref, b_ref, C):
    cols = slice(head * HG_D, (head + 1) * HG_D)
    hq = q_ref[0, :, cols]
    hf = f_ref[0, :, cols]
    v = i_ref[0, :, cols]
    gate = g_ref[0, :, cols]
    lb = lb_ref[:, cols]

    q = hq * _sigmoid(hq)
    e = jnp.exp(-jnp.abs(hf))
    la = jnp.log(lb)
    lc = jnp.log1p(-lb) + (jnp.minimum(hf, 0.0) - jnp.log(1.0 + e))
    log_f = jnp.maximum(la, lc) + jnp.log(1.0 + jnp.exp(-jnp.abs(la - lc)))
    k = (1.0 - lb) * (jnp.where(hf >= 0.0, e, 1.0) / (1.0 + e))

    b = _cumsum_rows_scan(log_f)
    b_ref[...] = b
    row = lax.broadcasted_iota(jnp.int32, (C, HG_D), 0)

    small = [n for n in (8, 4, 2) if n <= C]
    rr = lax.broadcasted_iota(jnp.int32, (C, C), 0)
    cc = lax.broadcasted_iota(jnp.int32, (C, C), 1)
    sel = jnp.concatenate(
        [jnp.where(cc == ((rr & ~(n - 1)) + n // 2 - 1), 1.0, 0.0).astype(BF16) for n in small], axis=0)
    b3 = jnp.concatenate(_split3(b), axis=1)
    g3 = jnp.dot(sel, b3, preferred_element_type=F32)
    b_mid_small = {}
    for i, n in enumerate(small):
        g = g3[i * C:(i + 1) * C]
        b_mid_small[n] = (g[:, 2 * HG_D:] + g[:, HG_D:2 * HG_D]) + g[:, :HG_D]

    att = jnp.zeros((C, C), F32)
    n = C
    while n >= 2:
        h = n // 2
        if n >= 16:
            pieces = [jnp.broadcast_to(b_ref[blk * n + h - 1:blk * n + h, :], (n, HG_D)) for blk in range(C // n)]
            b_mid = pieces[0] if len(pieces) == 1 else jnp.concatenate(pieces, axis=0)
        else:
            b_mid = b_mid_small[n]
        second = (row & (n - 1)) >= h
        ex = jnp.exp(jnp.where(second, b - b_mid, b_mid - b))
        q_n = jnp.where(second, q * ex, 0.0).astype(BF16)
        k_n = jnp.where(second, 0.0, k * ex).astype(BF16)
        a_n = lax.dot_general(q_n, k_n, (((1,), (1,)), ((), ())), preferred_element_type=F32)
        if n < C:
            shift = n.bit_length() - 1
            a_n = jnp.where((rr >> shift) == (cc >> shift), a_n, 0.0)
        att = att + a_n
        n = h

    ones = jnp.ones((HG_D, HG_D), BF16)
    o = jnp.dot((q * k).astype(BF16), ones, preferred_element_type=F32) * v

    s_t = sT_ref[...]
    v_bf = v.astype(BF16)
    o = o + lax.dot_general((q * jnp.exp(b)).astype(BF16), s_t.astype(BF16),
                            (((1,), (1,)), ((), ())), preferred_element_type=F32)
    o = o + jnp.dot(att.astype(BF16), v_bf, preferred_element_type=F32)

    b_last = b_ref[C - 1:C, :]
    k_dec = (k * jnp.exp(b_last - b)).astype(BF16)
    upd = lax.dot_general(v_bf, k_dec, (((0,), (0,)), ((), ())), preferred_element_type=F32)
    sT_ref[...] = s_t * jnp.exp(b_last) + upd

    ms = jnp.mean(o * o, axis=-1, keepdims=True)
    o = o * lax.rsqrt(ms + EPS) * nw_ref[...]
    o_ref[0, :, cols] = (o * (gate * _sigmoid(gate))).astype(o_ref.dtype)


def _hgrn(proj, lb, nw, s0):
    bsz, t, _ = proj.shape
    C = min(128, t)
    has_state = s0 is not None

    def col(nb):
        return pl.BlockSpec((1, C, HG_W), lambda b, c: (b, c, nb // HG_HEADS))

    in_specs = [col(NB_HG_Q), col(NB_HG_F), col(NB_HG_I), col(NB_HG_G),
                pl.BlockSpec((1, HG_W), lambda b, c: (0, 0)),
                pl.BlockSpec((1, HG_D), lambda b, c: (0, 0))]
    args = [proj, proj, proj, proj, lb.reshape(1, HG_W), nw.reshape(1, HG_D)]
    if has_state:
        in_specs.append(pl.BlockSpec((1, HG_HEADS, HG_D, HG_D), lambda b, c: (b, 0, 0, 0)))
        args.append(s0)
    return pl.pallas_call(
        functools.partial(_hgrn_kernel, C=C, has_state=has_state),
        grid=(bsz, t // C),
        in_specs=in_specs,
        out_specs=[pl.BlockSpec((1, C, HG_W), lambda b, c: (b, c, 0)),
                   pl.BlockSpec((1, HG_HEADS, HG_D, HG_D), lambda b, c: (b, 0, 0, 0))],
        out_shape=[jax.ShapeDtypeStruct((bsz, t, HG_W), BF16),
                   jax.ShapeDtypeStruct((bsz, HG_HEADS, HG_D, HG_D), F32)],
        scratch_shapes=[pltpu.VMEM((HG_D, HG_D), F32)] * HG_HEADS + [pltpu.VMEM((C, HG_D), F32)] * HG_HEADS,
        compiler_params=pltpu.CompilerParams(
            dimension_semantics=("parallel", "arbitrary"), vmem_limit_bytes=VMEM_LIMIT),
        name="hgrn2_scan",
    )(*args)


def _ssd_kernel(*refs, L, has_state):
    if has_state:
        (z_ref, xbc_ref, dt_ref, cw_ref, cb_ref, dtb_ref, al_ref, dsk_ref, nw_ref, conv0_ref, s0_ref,
         y_ref, st_ref, cst_ref, xp_ref, s_ref, ys_ref) = refs
    else:
        (z_ref, xbc_ref, dt_ref, cw_ref, cb_ref, dtb_ref, al_ref, dsk_ref, nw_ref,
         y_ref, st_ref, cst_ref, xp_ref, s_ref, ys_ref) = refs
    c = pl.program_id(1)

    @pl.when(c == 0)
    def _():
        xp_ref[0:SSD_PAD, :] = jnp.zeros((SSD_PAD, CONV_DIM), F32)
        if has_state:
            xp_ref[SSD_PAD - (CONV_W - 1):SSD_PAD, :] = conv0_ref[0]
            s_ref[...] = s0_ref[0].reshape(SSM_HEADS // 2, 2 * SSM_P, SSM_N)
        else:
            s_ref[...] = jnp.zeros((SSM_HEADS // 2, 2 * SSM_P, SSM_N), F32)

    xp_ref[SSD_PAD:SSD_PAD + L, :] = xbc_ref[0]
    cw = cw_ref[...]
    conv = cb_ref[...]
    for j in range(CONV_W):
        lo = SSD_PAD - (CONV_W - 1) + j
        conv = conv + cw[j:j + 1, :] * xp_ref[lo:lo + L, :]
    cst_ref[0] = xp_ref[L + SSD_PAD - (CONV_W - 1):L + SSD_PAD, :]
    xp_ref[0:SSD_PAD, :] = xp_ref[L:L + SSD_PAD, :]
    act = conv * _sigmoid(conv)
    xs = act[:, :SSM_INNER]

    dt_raw = dt_ref[0] + dtb_ref[...]
    dt = jnp.maximum(dt_raw, 0.0) + jnp.log(1.0 + jnp.exp(-jnp.abs(dt_raw)))
    a = -jnp.exp(al_ref[...])
    cs = _cumsum_rows(dt * a)
    cs_t = _transpose_rows(cs)
    dt_t = _transpose_rows(dt)

    tril = lax.broadcasted_iota(jnp.int32, (L, L), 0) >= lax.broadcasted_iota(jnp.int32, (L, L), 1)
    low_lane = lax.broadcasted_iota(jnp.int32, (L, LANE), 1) < SSM_P
    low_row = lax.broadcasted_iota(jnp.int32, (LANE, SSM_N), 0) < SSM_P
    hpg = SSM_HEADS // SSM_GROUPS
    for g in range(SSM_GROUPS):
        b_g = act[:, SSM_INNER + g * SSM_N:SSM_INNER + (g + 1) * SSM_N].astype(BF16)
        c_g = act[:, SSM_INNER + (SSM_GROUPS + g) * SSM_N:SSM_INNER + (SSM_GROUPS + g + 1) * SSM_N].astype(BF16)
        cb = lax.dot_general(c_g, b_g, (((1,), (1,)), ((), ())), preferred_element_type=F32)
        for pp in range(hpg // 2):
            p = (g * hpg) // 2 + pp
            x_p = xs[:, p * LANE:(p + 1) * LANE]
            x_bf = x_p.astype(BF16)
            s_p = s_ref[p]
            ws, e_col, wdec, e_last = [], [], [], []
            for h in (2 * p, 2 * p + 1):
                cs_col = cs[:, h:h + 1]
                cs_row = cs_t[h:h + 1, :]
                decay = jnp.exp(jnp.where(tril, cs_col - cs_row, -jnp.inf))
                ws.append((cb * decay * dt_t[h:h + 1, :]).astype(BF16))
                cs_last = cs_t[h:h + 1, L - 1:L]
                e_col.append(jnp.exp(cs_col))
                wdec.append(jnp.exp(cs_last - cs_col) * dt[:, h:h + 1])
                e_last.append(jnp.exp(cs_last))
            y_p = jnp.where(low_lane, jnp.dot(ws[0], x_bf, preferred_element_type=F32),
                            jnp.dot(ws[1], x_bf, preferred_element_type=F32))
            y_p = y_p + jnp.where(low_lane, e_col[0], e_col[1]) * lax.dot_general(
                c_g, s_p.astype(BF16), (((1,), (1,)), ((), ())), preferred_element_type=F32)
            ys_ref[:, p * LANE:(p + 1) * LANE] = y_p
            xw = (x_p * jnp.where(low_lane, wdec[0], wdec[1])).astype(BF16)
            upd = lax.dot_general(xw, b_g, (((0,), (0,)), ((), ())), preferred_element_type=F32)
            s_ref[p] = jnp.where(low_row, e_last[0], e_last[1]) * s_p + upd

    z = z_ref[0]
    y = (ys_ref[...] + dsk_ref[...] * xs) * (z * _sigmoid(z))
    gsz = SSM_INNER // SSM_GROUPS
    nw = nw_ref[...]
    for g in range(SSM_GROUPS):
        yg = y[:, g * gsz:(g + 1) * gsz]
        ms = jnp.mean(yg * yg, axis=-1, keepdims=True)
        y_ref[0, :, g * gsz:(g + 1) * gsz] = (yg * lax.rsqrt(ms + EPS) * nw[:, g * gsz:(g + 1) * gsz]).astype(y_ref.dtype)

    @pl.when(c == pl.num_programs(1) - 1)
    def _():
        st_ref[0] = s_ref[...].reshape(SSM_HEADS, SSM_P, SSM_N)


def _pad_lanes(v):
    return jnp.pad(v.astype(F32), (0, LANE - v.shape[0])).reshape(1, LANE)


def _ssd(proj, conv_w, conv_b, dt_bias, a_log, d_skip, ssm_norm_w, conv0, s0):
    bsz, t, _ = proj.shape
    L = min(256, t)
    has_state = s0 is not None

    def const(shape):
        return pl.BlockSpec(shape, lambda b, c: (0,) * len(shape))

    in_specs = [pl.BlockSpec((1, L, SSM_INNER), lambda b, c: (b, c, NB_Z * LANE // SSM_INNER)),
                pl.BlockSpec((1, L, CONV_DIM), lambda b, c: (b, c, NB_XBC * LANE // CONV_DIM)),
                pl.BlockSpec((1, L, LANE), lambda b, c: (b, c, NB_DT)),
                const((CONV_W, CONV_DIM)), const((1, CONV_DIM)), const((1, LANE)), const((1, LANE)),
                const((1, SSM_INNER)), const((1, SSM_INNER))]
    args = [proj, proj, proj, conv_w, conv_b.reshape(1, CONV_DIM), _pad_lanes(dt_bias), _pad_lanes(a_log),
            jnp.repeat(d_skip.astype(F32), SSM_P).reshape(1, SSM_INNER), ssm_norm_w.reshape(1, SSM_INNER)]
    if has_state:
        in_specs += [pl.BlockSpec((1, CONV_W - 1, CONV_DIM), lambda b, c: (b, 0, 0)),
                     pl.BlockSpec((1, SSM_HEADS, SSM_P, SSM_N), lambda b, c: (b, 0, 0, 0))]
        args += [conv0, s0]
    return pl.pallas_call(
        functools.partial(_ssd_kernel, L=L, has_state=has_state),
        grid=(bsz, t // L),
        in_specs=in_specs,
        out_specs=[pl.BlockSpec((1, L, SSM_INNER), lambda b, c: (b, c, 0)),
                   pl.BlockSpec((1, SSM_HEADS, SSM_P, SSM_N), lambda b, c: (b, 0, 0, 0)),
                   pl.BlockSpec((1, CONV_W - 1, CONV_DIM), lambda b, c: (b, 0, 0))],
        out_shape=[jax.ShapeDtypeStruct((bsz, t, SSM_INNER), BF16),
                   jax.ShapeDtypeStruct((bsz, SSM_HEADS, SSM_P, SSM_N), F32),
                   jax.ShapeDtypeStruct((bsz, CONV_W - 1, CONV_DIM), F32)],
        scratch_shapes=[pltpu.VMEM((L + SSD_PAD, CONV_DIM), F32),
                        pltpu.VMEM((SSM_HEADS // 2, 2 * SSM_P, SSM_N), F32),
                        pltpu.VMEM((L, SSM_INNER), F32)],
        compiler_params=pltpu.CompilerParams(
            dimension_semantics=("parallel", "arbitrary"), vmem_limit_bytes=VMEM_LIMIT),
        name="conv_ssd_scan",
    )(*args)


def _neg_tri_ext(kb):
    kk = lax.broadcasted_iota(jnp.int32, (kb, kb + LANE), 0)
    jj = lax.broadcasted_iota(jnp.int32, (kb, kb + LANE), 1)
    return jnp.where((kk >= jj) | (jj >= kb), -1.0, 0.0).astype(BF16)


def _sb_kernel(*refs, QB, KP, n_past):
    if n_past:
        q_ref, k_ref, v_ref, g_ref, kp_ref, vp_ref, o_ref = refs[:7]
        scr = refs[7:]
    else:
        q_ref, k_ref, v_ref, g_ref, o_ref = refs[:5]
        scr = refs[5:]
    nh = SB_HEADS
    acc_refs, r_refs, xn_ref, tn_ref = scr[0:nh], scr[nh:2 * nh], scr[2 * nh:3 * nh], scr[3 * nh:4 * nh]
    if n_past:
        xp_ref, tp_ref = scr[4 * nh:5 * nh], scr[5 * nh:6 * nh]
    qi = pl.program_id(1)
    nt = (((1,), (1,)), ((), ()))

    def lanes(h):
        return slice(h * SB_DH, (h + 1) * SB_DH)

    def qk(h, k_t):
        return lax.dot_general(q_ref[0, :, lanes(h)], k_t, nt, preferred_element_type=F32)

    def scores(h, z, ntri, x_ref, t_ref, mask=None):
        kb = z.shape[1]
        sp = jnp.maximum(z, 0.0) + jnp.log2(1.0 + jnp.exp2(-jnp.abs(z)))
        if mask is not None:
            sp = jnp.where(mask, sp, 0.0)
        wt = jnp.dot(sp.astype(BF16), ntri, preferred_element_type=F32)
        x = z + wt[:, :kb]
        if mask is not None:
            x = jnp.where(mask, x, -jnp.inf)
        x_ref[h][...] = x
        t_ref[h][...] = wt[:, kb:]

    def accumulate(h, v_t, x_ref, t_ref):
        kb = v_t.shape[0]
        r = r_refs[h][...]
        w = jnp.exp2(x_ref[h][...] + jnp.tile(r, (1, max(kb // LANE, 1)))[:, :kb])
        acc_refs[h][...] += jnp.dot(w.astype(BF16), v_t, preferred_element_type=F32)
        r_refs[h][...] = r + t_ref[h][...]

    for h in range(SB_HEADS):
        acc_refs[h][...] = jnp.zeros_like(acc_refs[h])
        r_refs[h][...] = jnp.zeros_like(r_refs[h])

    start = pl.multiple_of(qi * QB, QB)
    mask = lax.broadcasted_iota(jnp.int32, (QB, QB), 1) < lax.broadcasted_iota(jnp.int32, (QB, QB), 0)
    ntri_q = _neg_tri_ext(QB)
    for h in range(SB_HEADS):
        scores(h, qk(h, k_ref[0, pl.ds(start, QB), lanes(h)]), ntri_q, xn_ref, tn_ref, mask=mask)

    if QB % LANE == 0:
        def new_body(jj, carry):
            s = pl.multiple_of((qi - 1 - jj) * QB, QB)
            zs = [qk(h, k_ref[0, pl.ds(s, QB), lanes(h)]) for h in range(SB_HEADS)]
            for h in range(SB_HEADS):
                accumulate(h, v_ref[0, pl.ds(s + QB, QB), lanes(h)], xn_ref, tn_ref)
            for h in range(SB_HEADS):
                scores(h, zs[h], ntri_q, xn_ref, tn_ref)
            return carry

        lax.fori_loop(0, qi, new_body, 0)

    if n_past:
        ntri_p = _neg_tri_ext(KP)
        last = (n_past - 1) * KP
        for h in range(SB_HEADS):
            accumulate(h, v_ref[0, 0:QB, lanes(h)], xn_ref, tn_ref)
            scores(h, qk(h, kp_ref[0, last:last + KP, lanes(h)].astype(BF16)), ntri_p, xp_ref, tp_ref)

        def past_body(jj, carry):
            s = pl.multiple_of((n_past - 1 - jj) * KP, KP)
            zs = [qk(h, kp_ref[0, pl.ds(s, KP), lanes(h)].astype(BF16)) for h in range(SB_HEADS)]
            for h in range(SB_HEADS):
                accumulate(h, vp_ref[0, pl.ds(s + KP, KP), lanes(h)].astype(BF16), xp_ref, tp_ref)
            for h in range(SB_HEADS):
                scores(h, zs[h], ntri_p, xp_ref, tp_ref)
            return carry

        lax.fori_loop(1, n_past, past_body, 0)
        for h in range(SB_HEADS):
            accumulate(h, vp_ref[0, 0:KP, lanes(h)].astype(BF16), xp_ref, tp_ref)
    else:
        for h in range(SB_HEADS):
            accumulate(h, v_ref[0, 0:QB, lanes(h)], xn_ref, tn_ref)

    gate = g_ref[0]
    for h in range(SB_HEADS):
        gh = gate[:, lanes(h)]
        o_ref[0, :, lanes(h)] = (acc_refs[h][...] * (gh * _sigmoid(gh))).astype(o_ref.dtype)


def _neg_tri_ext_t(kb):
    jj = lax.broadcasted_iota(jnp.int32, (kb + 16, kb), 0)
    kk = lax.broadcasted_iota(jnp.int32, (kb + 16, kb), 1)
    return jnp.where((kk >= jj) | (jj >= kb), -1.0, 0.0).astype(BF16)


def _sbt_kernel(q_ref, k_ref, v_ref, g_ref, o_ref, *scr, QB):
    nh = SB_HEADS
    acc_refs, r_refs, x_refs, t_refs = scr[0:nh], scr[nh:2 * nh], scr[2 * nh:3 * nh], scr[3 * nh:4 * nh]
    qi = pl.program_id(1)
    nt = (((1,), (1,)), ((), ()))
    tn = (((0,), (0,)), ((), ()))
    ntri = _neg_tri_ext_t(QB)

    def lanes(h):
        return slice(h * SB_DH, (h + 1) * SB_DH)

    def qk(h, k_t):
        return lax.dot_general(k_t, q_ref[0, :, lanes(h)], nt, preferred_element_type=F32)

    def scores(h, z, mask=None):
        sp = jnp.maximum(z, 0.0) + jnp.log2(1.0 + jnp.exp2(-jnp.abs(z)))
        if mask is not None:
            sp = jnp.where(mask, sp, 0.0)
        wt = jnp.dot(ntri, sp.astype(BF16), preferred_element_type=F32)
        x = z + wt[:QB]
        if mask is not None:
            x = jnp.where(mask, x, -jnp.inf)
        x_refs[h][...] = x
        t_refs[h][...] = wt[QB:QB + 8]

    def accumulate(h, v_t):
        r = r_refs[h][...]
        w = jnp.exp2(x_refs[h][...] + r[0:1, :])
        acc_refs[h][...] += lax.dot_general(v_t, w.astype(BF16), tn, preferred_element_type=F32)
        r_refs[h][...] = r + t_refs[h][...]

    for h in range(nh):
        acc_refs[h][...] = jnp.zeros_like(acc_refs[h])
        r_refs[h][...] = jnp.zeros_like(r_refs[h])

    start = pl.multiple_of(qi * QB, QB)
    mask = lax.broadcasted_iota(jnp.int32, (QB, QB), 0) < lax.broadcasted_iota(jnp.int32, (QB, QB), 1)
    for h in range(nh):
        scores(h, qk(h, k_ref[0, pl.ds(start, QB), lanes(h)]), mask=mask)

    def body(jj, carry):
        s = pl.multiple_of((qi - 1 - jj) * QB, QB)
        zs = [qk(h, k_ref[0, pl.ds(s, QB), lanes(h)]) for h in range(nh)]
        for h in range(nh):
            accumulate(h, v_ref[0, pl.ds(s + QB, QB), lanes(h)])
        for h in range(nh):
            scores(h, zs[h])
        return carry

    lax.fori_loop(0, qi, body, 0)
    gate = g_ref[0]
    for h in range(nh):
        accumulate(h, v_ref[0, 0:QB, lanes(h)])
        gh = gate[:, lanes(h)]
        o_ref[0, :, lanes(h)] = (acc_refs[h][...].T * (gh * _sigmoid(gh))).astype(o_ref.dtype)


def _stick_breaking(rest, qkv, k_past, v_past, layer):
    bsz, t, _ = rest.shape
    QB = min(256, t)
    KP = 256
    n_past = 0 if k_past is None else k_past.shape[2] // KP
    assert t % QB == 0 and (t == QB or QB % LANE == 0)

    in_specs = [pl.BlockSpec((1, QB, SB_W), lambda b, i: (b, i, 0)),
                pl.BlockSpec((1, t, SB_W), lambda b, i: (b, 0, 1)),
                pl.BlockSpec((1, t, SB_W), lambda b, i: (b, 0, 2)),
                pl.BlockSpec((1, QB, SB_W), lambda b, i: (b, i, NB_SB_G * LANE // SB_W))]
    args = [qkv, qkv, qkv, rest]
    if n_past:
        assert k_past.shape[2] % KP == 0
        past = pl.BlockSpec((None, 1, k_past.shape[2], SB_W), lambda b, i: (layer, b, 0, 0))
        in_specs += [past, past]
        args += [k_past, v_past]
    if not n_past and QB % LANE == 0:
        return pl.pallas_call(
            functools.partial(_sbt_kernel, QB=QB),
            grid=(bsz, t // QB),
            in_specs=in_specs,
            out_specs=pl.BlockSpec((1, QB, SB_W), lambda b, i: (b, i, 0)),
            out_shape=jax.ShapeDtypeStruct((bsz, t, SB_W), BF16),
            scratch_shapes=[pltpu.VMEM((SB_DH, QB), F32)] * SB_HEADS + [pltpu.VMEM((8, QB), F32)] * SB_HEADS
            + [pltpu.VMEM((QB, QB), F32)] * SB_HEADS + [pltpu.VMEM((8, QB), F32)] * SB_HEADS,
            compiler_params=pltpu.CompilerParams(
                dimension_semantics=("parallel", "arbitrary"), vmem_limit_bytes=VMEM_LIMIT),
            name="stick_breaking_t",
        )(*args)
    return pl.pallas_call(
        functools.partial(_sb_kernel, QB=QB, KP=KP, n_past=n_past),
        grid=(bsz, t // QB),
        in_specs=in_specs,
        out_specs=pl.BlockSpec((1, QB, SB_W), lambda b, i: (b, i, 0)),
        out_shape=jax.ShapeDtypeStruct((bsz, t, SB_W), BF16),
        scratch_shapes=[pltpu.VMEM((QB, SB_DH), F32)] * (2 * SB_HEADS) + [pltpu.VMEM((QB, QB), F32)] * SB_HEADS
        + [pltpu.VMEM((QB, LANE), F32)] * SB_HEADS
        + ([pltpu.VMEM((QB, KP), F32)] * SB_HEADS + [pltpu.VMEM((QB, LANE), F32)] * SB_HEADS if n_past else []),
        compiler_params=pltpu.CompilerParams(
            dimension_semantics=("parallel", "arbitrary"), vmem_limit_bytes=VMEM_LIMIT),
        name="stick_breaking",
    )(*args)


def _outproj_kernel(*refs, final):
    if final:
        x_ref, a_ref, b_ref, c_ref, w_ref, fw_ref, o_ref = refs
    else:
        x_ref, a_ref, b_ref, c_ref, w_ref, o_ref = refs
    y = jnp.dot(a_ref[...], w_ref[0:HG_W, :], preferred_element_type=F32)
    y = y + jnp.dot(b_ref[...], w_ref[HG_W:HG_W + SSM_INNER, :], preferred_element_type=F32)
    y = y + jnp.dot(c_ref[...], w_ref[HG_W + SSM_INNER:, :], preferred_element_type=F32)
    x = x_ref[...] + y
    if final:
        ms = jnp.mean(x * x, axis=-1, keepdims=True)
        x = x * lax.rsqrt(ms + EPS) * fw_ref[...]
    o_ref[...] = x


def _out_proj(x2d, o_hg, y_ssm, o_sb, w_bf, final_w):
    m, d = x2d.shape
    tm = min(1024, m)
    final = final_w is not None

    def rows(w):
        return pl.BlockSpec((tm, w), lambda i: (i, 0))

    in_specs = [rows(d), rows(HG_W), rows(SSM_INNER), rows(SB_W),
                pl.BlockSpec(w_bf.shape, lambda i: (0, 0))]
    args = [x2d, o_hg, y_ssm, o_sb, w_bf]
    if final:
        in_specs.append(pl.BlockSpec((1, d), lambda i: (0, 0)))
        args.append(final_w.reshape(1, d))
    return pl.pallas_call(
        functools.partial(_outproj_kernel, final=final),
        grid=(m // tm,),
        in_specs=in_specs,
        out_specs=rows(d),
        out_shape=jax.ShapeDtypeStruct((m, d), F32),
        compiler_params=pltpu.CompilerParams(
            dimension_semantics=("parallel",), vmem_limit_bytes=VMEM_LIMIT),
        name="out_proj",
    )(*args)


def _reorder_w_in(w_in):
    hg_cols = 4 * HG_W
    ssm_cols = SSM_INNER + CONV_DIM + SSM_HEADS
    hg = w_in[..., :hg_cols]
    z_xbc = w_in[..., hg_cols:hg_cols + SSM_INNER + CONV_DIM]
    dt = w_in[..., hg_cols + SSM_INNER + CONV_DIM:hg_cols + ssm_cols]
    sb = w_in[..., hg_cols + ssm_cols:]
    sb_q = sb[..., :SB_W] * (LOG2E * SB_DH ** -0.5)
    sb_kv = sb[..., SB_W:3 * SB_W]
    sb_g = sb[..., 3 * SB_W:]
    z = z_xbc[..., :SSM_INNER]
    xbc = z_xbc[..., SSM_INNER:]
    used = w_in.shape[-1]
    pad = jnp.zeros(w_in.shape[:-1] + (NP - used,), w_in.dtype)
    return jnp.concatenate([sb_q, sb_kv, sb_g, xbc, dt, pad, z, hg], axis=-1).astype(BF16)


def _lower_bounds(lb_logits):
    p = jax.nn.softmax(lb_logits.astype(F32), axis=0)
    return jnp.clip(jnp.cumsum(p, axis=0) - p[0:1], 0.0, 1.0)


def _layer(x, states, kv_all, layer, depth, lb, norm_w, w_in_bf, conv_w, conv_b, dt_bias, a_log, d_skip,
           ssm_norm_w, hg_norm_w, w_out_bf, final_w):
    bsz, t, d = x.shape
    conv0, ssm0, hg0, k_past, v_past = states
    x2d = x.reshape(bsz * t, d)
    rest, qkv, k_all, v_all = _in_proj(x2d, norm_w, w_in_bf, layer, depth, *kv_all)
    rest = rest.reshape(bsz, t, NR)
    qkv = qkv.reshape(bsz, t, QKV_W)
    o_hg, hg_state = _hgrn(rest, lb, hg_norm_w, hg0)
    y_ssm, ssm_state, conv_state = _ssd(rest, conv_w, conv_b, dt_bias, a_log, d_skip, ssm_norm_w, conv0, ssm0)
    o_sb = _stick_breaking(rest, qkv, k_past, v_past, layer)
    x_new = _out_proj(x2d, o_hg.reshape(bsz * t, HG_W), y_ssm.reshape(bsz * t, SSM_INNER),
                      o_sb.reshape(bsz * t, SB_W), w_out_bf, final_w).reshape(bsz, t, d)
    return x_new, (hg_state, ssm_state, conv_state), (k_all, v_all)


def _run_group(x, layer_states, lbs, w_in_bf, w_out_bf, params, final_norm_w):
    norm_w, conv_w, conv_b, dt_bias, a_log, d_skip, ssm_norm_w, hg_norm_w = params
    depth = norm_w.shape[0]
    bsz, t, _ = x.shape
    outs = []
    kv_all = (None, None)
    for l in range(depth):
        x, st, kv_all = _layer(x, layer_states(l), kv_all, l, depth, lbs[l], norm_w[l], w_in_bf[l], conv_w[l],
                               conv_b[l], dt_bias[l], a_log[l], d_skip[l], ssm_norm_w[l], hg_norm_w[l],
                               w_out_bf[l], final_norm_w if l == depth - 1 else None)
        outs.append(st)
    kv = tuple(a.reshape(depth, bsz, t, SB_HEADS, SB_DH) for a in kv_all)
    return x, tuple(jnp.stack([st[i] for st in outs]) for i in range(3)) + kv


def kernel(x_prompt, x_sample, state_hgrn, state_ssm, state_conv, cache_k, cache_v, norm_w, w_in, lb_logits,
           conv_w, conv_b, dt_bias, a_log, d_skip, ssm_norm_w, hg_norm_w, w_out, final_norm_w):
    lbs = _lower_bounds(lb_logits)
    w_in_bf = _reorder_w_in(w_in)
    w_out_bf = w_out.astype(BF16)
    params = (norm_w, conv_w, conv_b, dt_bias, a_log, d_skip, ssm_norm_w, hg_norm_w)

    y_prompt, p_st = _run_group(x_prompt, lambda l: (None, None, None, None, None),
                                lbs, w_in_bf, w_out_bf, params, final_norm_w)
    k_cache = cache_k.reshape(cache_k.shape[:3] + (SB_W,))
    v_cache = cache_v.reshape(cache_v.shape[:3] + (SB_W,))
    y_sample, s_st = _run_group(x_sample,
                                lambda l: (state_conv[l], state_ssm[l], state_hgrn[l], k_cache, v_cache),
                                lbs, w_in_bf, w_out_bf, params, final_norm_w)
    return (y_prompt, y_sample) + p_st + s_st
```

```python
import functools

import jax
import jax.numpy as jnp
from jax import lax
from jax.experimental import pallas as pl
from jax.experimental.pallas import tpu as pltpu

F32 = jnp.float32
BF16 = jnp.bfloat16
EPS = 1e-6

LANE = 128
HG_HEADS = 4
HG_D = 128
SSM_HEADS = 16
SSM_P = 64
SSM_N = 128
SSM_GROUPS = 2
SSM_INNER = SSM_HEADS * SSM_P
CONV_W = 4
CONV_DIM = SSM_INNER + 2 * SSM_GROUPS * SSM_N
SB_HEADS = 4
SB_DH = 128
HG_W = HG_HEADS * HG_D
SB_W = SB_HEADS * SB_DH

PROJ_TN = 8 * LANE
QKV_TILES = 2
QKV_W = QKV_TILES * PROJ_TN
NB_XBC = 0
NB_DT = 12
NB_Z = 16
NB_HG_Q, NB_HG_F, NB_HG_I, NB_HG_G = 24, 28, 32, 36
NB_SB_G = 44
REST_TILES = 6
NR = REST_TILES * PROJ_TN
NP = 7 * PROJ_TN
LOG2E = 1.4426950408889634

VMEM_LIMIT = 56 * 1024 * 1024

SSD_PAD = 8


def _sigmoid(x):
    return 1.0 / (1.0 + jnp.exp(-x))


def _cumsum_rows_scan(x):
    n = x.shape[0]
    row = lax.broadcasted_iota(jnp.int32, x.shape, 0)
    sh = 1
    while sh < n:
        x = x + jnp.where(row >= sh, pltpu.roll(x, sh, axis=0), 0.0)
        sh *= 2
    return x


def _split3(x):
    hi = x.astype(BF16)
    r1 = x - hi.astype(F32)
    mid = r1.astype(BF16)
    lo = (r1 - mid.astype(F32)).astype(BF16)
    return hi, mid, lo


def _cumsum_rows(x):
    n = x.shape[0]
    tri = (lax.broadcasted_iota(jnp.int32, (n, n), 0) >= lax.broadcasted_iota(jnp.int32, (n, n), 1))
    tri = jnp.where(tri, 1.0, 0.0).astype(BF16)
    hi, mid, lo = _split3(x)
    out = jnp.dot(tri, lo, preferred_element_type=F32)
    out = out + jnp.dot(tri, mid, preferred_element_type=F32)
    return out + jnp.dot(tri, hi, preferred_element_type=F32)


def _transpose_rows(x):
    n = x.shape[0]
    if n < LANE:
        x = jnp.concatenate([x, jnp.zeros((LANE - n, LANE), x.dtype)], axis=0)
    return x.T[:, :n]


def _inproj_kernel(x_ref, nw_ref, w_ref, k_in_ref, v_in_ref, rest_ref, qkv_ref, k_ref, v_ref, xn_ref):
    del k_in_ref, v_in_ref
    j = pl.program_id(1)

    @pl.when(j == 0)
    def _():
        x = x_ref[...]
        ms = jnp.mean(x * x, axis=-1, keepdims=True)
        xn_ref[...] = (x * lax.rsqrt(ms + EPS) * nw_ref[...]).astype(BF16)

    acc = jnp.dot(xn_ref[...], w_ref[...], preferred_element_type=F32)

    @pl.when(j < QKV_TILES)
    def _():
        qkv_ref[...] = acc.astype(BF16)

    @pl.when(j == 0)
    def _():
        k_ref[0] = acc[:, SB_W:]

    @pl.when(j == 1)
    def _():
        v_ref[0] = acc[:, :SB_W]

    @pl.when(j >= 1)
    def _():
        rest_ref[...] = acc


def _in_proj(x2d, norm_w, w_bf, layer, depth, k_all, v_all):
    m, d = x2d.shape
    tm = min(1024, m)
    in_specs = [
        pl.BlockSpec((tm, d), lambda i, j: (i, 0)),
        pl.BlockSpec((1, d), lambda i, j: (0, 0)),
        pl.BlockSpec((d, PROJ_TN), lambda i, j: (0, j)),
        pl.BlockSpec(memory_space=pl.ANY),
        pl.BlockSpec(memory_space=pl.ANY),
    ]
    args = [x2d, norm_w.reshape(1, d), w_bf, k_all, v_all]
    kv_spec = pl.BlockSpec((1, tm, SB_W), lambda i, j: (layer, i, 0))
    kv_shape = jax.ShapeDtypeStruct((depth, m, SB_W), F32)
    return pl.pallas_call(
        _inproj_kernel,
        grid=(m // tm, NP // PROJ_TN),
        in_specs=in_specs,
        out_specs=[pl.BlockSpec((tm, PROJ_TN), lambda i, j: (i, jnp.where(j < QKV_TILES, REST_TILES - 1, j - QKV_TILES))),
                   pl.BlockSpec((tm, PROJ_TN), lambda i, j: (i, jnp.minimum(j, QKV_TILES - 1))),
                   kv_spec, kv_spec],
        out_shape=[jax.ShapeDtypeStruct((m, NR), F32),
                   jax.ShapeDtypeStruct((m, QKV_W), BF16),
                   kv_shape, kv_shape],
        input_output_aliases={3: 2, 4: 3},
        scratch_shapes=[pltpu.VMEM((tm, d), BF16)],
        compiler_params=pltpu.CompilerParams(
            dimension_semantics=("parallel", "arbitrary"), vmem_limit_bytes=VMEM_LIMIT),
        name="in_proj",
    )(*args)


def _hgrn_kernel(*refs, C, has_state):
    if has_state:
        q_ref, f_ref, i_ref, g_ref, lb_ref, nw_ref, s0_ref, o_ref, st_ref = refs[:9]
        scr = refs[9:]
    else:
        q_ref, f_ref, i_ref, g_ref, lb_ref, nw_ref, o_ref, st_ref = refs[:8]
        scr = refs[8:]
    sT_refs, b_refs = scr[:HG_HEADS], scr[HG_HEADS:]
    c = pl.program_id(1)

    @pl.when(c == 0)
    def _():
        for h in range(HG_HEADS):
            if has_state:
                sT_refs[h][...] = s0_ref[0, h].T
            else:
                sT_refs[h][...] = jnp.zeros((HG_D, HG_D), F32)

    for h in range(HG_HEADS):
        _hgrn_head(h, q_ref, f_ref, i_ref, g_ref, lb_ref, nw_ref, o_ref, sT_refs[h], b_refs[h], C)

    @pl.when(c == pl.num_programs(1) - 1)
    def _():
        for h in range(HG_HEADS):
            st_ref[0, h] = sT_refs[h][...].T


def _hgrn_head(head, q_ref, f_ref, i_ref, g_ref, lb_ref, nw_ref, o_ref, sT_ref, b_ref, C):
    cols = slice(head * HG_D, (head + 1) * HG_D)
    hq = q_ref[0, :, cols]
    hf = f_ref[0, :, cols]
    v = i_ref[0, :, cols]
    gate = g_ref[0, :, cols]
    lb = lb_ref[:, cols]

    q = hq * _sigmoid(hq)
    e = jnp.exp(-jnp.abs(hf))
    la = jnp.log(lb)
    lc = jnp.log1p(-lb) + (jnp.minimum(hf, 0.0) - jnp.log(1.0 + e))
    log_f = jnp.maximum(la, lc) + jnp.log(1.0 + jnp.exp(-jnp.abs(la - lc)))
    k = (1.0 - lb) * (jnp.where(hf >= 0.0, e, 1.0) / (1.0 + e))

    b = _cumsum_rows_scan(log_f)
    b_ref[...] = b
    row = lax.broadcasted_iota(jnp.int32, (C, HG_D), 0)

    small = [n for n in (8, 4, 2) if n <= C]
    rr = lax.broadcasted_iota(jnp.int32, (C, C), 0)
    cc = lax.broadcasted_iota(jnp.int32, (C, C), 1)
    sel = jnp.concatenate(
        [jnp.where(cc == ((rr & ~(n - 1)) + n // 2 - 1), 1.0, 0.0).astype(BF16) for n in small], axis=0)
    b3 = jnp.concatenate(_split3(b), axis=1)
    g3 = jnp.dot(sel, b3, preferred_element_type=F32)
    b_mid_small = {}
    for i, n in enumerate(small):
        g = g3[i * C:(i + 1) * C]
        b_mid_small[n] = (g[:, 2 * HG_D:] + g[:, HG_D:2 * HG_D]) + g[:, :HG_D]

    att = jnp.zeros((C, C), F32)
    n = C
    while n >= 2:
        h = n // 2
        if n >= 16:
            pieces = [jnp.broadcast_to(b_ref[blk * n + h - 1:blk * n + h, :], (n, HG_D)) for blk in range(C // n)]
            b_mid = pieces[0] if len(pieces) == 1 else jnp.concatenate(pieces, axis=0)
        else:
            b_mid = b_mid_small[n]
        second = (row & (n - 1)) >= h
        ex = jnp.exp(jnp.where(second, b - b_mid, b_mid - b))
        q_n = jnp.where(second, q * ex, 0.0).astype(BF16)
        k_n = jnp.where(second, 0.0, k * ex).astype(BF16)
        a_n = lax.dot_general(q_n, k_n, (((1,), (1,)), ((), ())), preferred_element_type=F32)
        if n < C:
            shift = n.bit_length() - 1
            a_n = jnp.where((rr >> shift) == (cc >> shift), a_n, 0.0)
        att = att + a_n
        n = h

    ones = jnp.ones((HG_D, HG_D), BF16)
    o = jnp.dot((q * k).astype(BF16), ones, preferred_element_type=F32) * v

    s_t = sT_ref[...]
    v_bf = v.astype(BF16)
    o = o + lax.dot_general((q * jnp.exp(b)).astype(BF16), s_t.astype(BF16),
                            (((1,), (1,)), ((), ())), preferred_element_type=F32)
    o = o + jnp.dot(att.astype(BF16), v_bf, preferred_element_type=F32)

    b_last = b_ref[C - 1:C, :]
    k_dec = (k * jnp.exp(b_last - b)).astype(BF16)
    upd = lax.dot_general(v_bf, k_dec, (((0,), (0,)), ((), ())), preferred_element_type=F32)
    sT_ref[...] = s_t * jnp.exp(b_last) + upd

    ms = jnp.mean(o * o, axis=-1, keepdims=True)
    o = o * lax.rsqrt(ms + EPS) * nw_ref[...]
    o_ref[0, :, cols] = (o * (gate * _sigmoid(gate))).astype(o_ref.dtype)


def _hgrn(proj, lb, nw, s0):
    bsz, t, _ = proj.shape
    C = min(128, t)
    has_state = s0 is not None

    def col(nb):
        return pl.BlockSpec((1, C, HG_W), lambda b, c: (b, c, nb // HG_HEADS))

    in_specs = [col(NB_HG_Q), col(NB_HG_F), col(NB_HG_I), col(NB_HG_G),
                pl.BlockSpec((1, HG_W), lambda b, c: (0, 0)),
                pl.BlockSpec((1, HG_D), lambda b, c: (0, 0))]
    args = [proj, proj, proj, proj, lb.reshape(1, HG_W), nw.reshape(1, HG_D)]
    if has_state:
        in_specs.append(pl.BlockSpec((1, HG_HEADS, HG_D, HG_D), lambda b, c: (b, 0, 0, 0)))
        args.append(s0)
    return pl.pallas_call(
        functools.partial(_hgrn_kernel, C=C, has_state=has_state),
        grid=(bsz, t // C),
        in_specs=in_specs,
        out_specs=[pl.BlockSpec((1, C, HG_W), lambda b, c: (b, c, 0)),
                   pl.BlockSpec((1, HG_HEADS, HG_D, HG_D), lambda b, c: (b, 0, 0, 0))],
        out_shape=[jax.ShapeDtypeStruct((bsz, t, HG_W), BF16),
                   jax.ShapeDtypeStruct((bsz, HG_HEADS, HG_D, HG_D), F32)],
        scratch_shapes=[pltpu.VMEM((HG_D, HG_D), F32)] * HG_HEADS + [pltpu.VMEM((C, HG_D), F32)] * HG_HEADS,
        compiler_params=pltpu.CompilerParams(
            dimension_semantics=("parallel", "arbitrary"), vmem_limit_bytes=VMEM_LIMIT),
        name="hgrn2_scan",
    )(*args)


def _ssd_kernel(*refs, L, has_state):
    if has_state:
        (z_ref, xbc_ref, dt_ref, cw_ref, cb_ref, dtb_ref, al_ref, dsk_ref, nw_ref, conv0_ref, s0_ref,
         y_ref, st_ref, cst_ref, xp_ref, s_ref, ys_ref) = refs
    else:
        (z_ref, xbc_ref, dt_ref, cw_ref, cb_ref, dtb_ref, al_ref, dsk_ref, nw_ref,
         y_ref, st_ref, cst_ref, xp_ref, s_ref, ys_ref) = refs
    c = pl.program_id(1)

    @pl.when(c == 0)
    def _():
        xp_ref[0:SSD_PAD, :] = jnp.zeros((SSD_PAD, CONV_DIM), F32)
        if has_state:
            xp_ref[SSD_PAD - (CONV_W - 1):SSD_PAD, :] = conv0_ref[0]
            s_ref[...] = s0_ref[0].reshape(SSM_HEADS // 2, 2 * SSM_P, SSM_N)
        else:
            s_ref[...] = jnp.zeros((SSM_HEADS // 2, 2 * SSM_P, SSM_N), F32)

    xp_ref[SSD_PAD:SSD_PAD + L, :] = xbc_ref[0]
    cw = cw_ref[...]
    conv = cb_ref[...]
    for j in range(CONV_W):
        lo = SSD_PAD - (CONV_W - 1) + j
        conv = conv + cw[j:j + 1, :] * xp_ref[lo:lo + L, :]
    cst_ref[0] = xp_ref[L + SSD_PAD - (CONV_W - 1):L + SSD_PAD, :]
    xp_ref[0:SSD_PAD, :] = xp_ref[L:L + SSD_PAD, :]
    act = conv * _sigmoid(conv)
    xs = act[:, :SSM_INNER]

    dt_raw = dt_ref[0] + dtb_ref[...]
    dt = jnp.maximum(dt_raw, 0.0) + jnp.log(1.0 + jnp.exp(-jnp.abs(dt_raw)))
    a = -jnp.exp(al_ref[...])
    cs = _cumsum_rows(dt * a)
    cs_t = _transpose_rows(cs)
    cs2 = cs * LOG2E
    row2 = _transpose_rows(cs2 - jnp.log2(dt))

    tril = lax.broadcasted_iota(jnp.int32, (L, L), 0) >= lax.broadcasted_iota(jnp.int32, (L, L), 1)
    low_lane = lax.broadcasted_iota(jnp.int32, (L, LANE), 1) < SSM_P
    low_row = lax.broadcasted_iota(jnp.int32, (LANE, SSM_N), 0) < SSM_P
    hpg = SSM_HEADS // SSM_GROUPS
    for g in range(SSM_GROUPS):
        b_g = act[:, SSM_INNER + g * SSM_N:SSM_INNER + (g + 1) * SSM_N].astype(BF16)
        c_g = act[:, SSM_INNER + (SSM_GROUPS + g) * SSM_N:SSM_INNER + (SSM_GROUPS + g + 1) * SSM_N].astype(BF16)
        cb = lax.dot_general(c_g, b_g, (((1,), (1,)), ((), ())), preferred_element_type=F32)
        for pp in range(hpg // 2):
            p = (g * hpg) // 2 + pp
            x_p = xs[:, p * LANE:(p + 1) * LANE]
            x_bf = x_p.astype(BF16)
            s_p = s_ref[p]
            ws, e_col, wdec, e_last = [], [], [], []
            for h in (2 * p, 2 * p + 1):
                cs_col = cs[:, h:h + 1]
                cs_row = cs_t[h:h + 1, :]
                decay = jnp.exp2(jnp.where(tril, cs2[:, h:h + 1] - row2[h:h + 1, :], -jnp.inf))
                ws.append((cb * decay).astype(BF16))
                cs_last = cs_t[h:h + 1, L - 1:L]
                e_col.append(jnp.exp(cs_col))
                wdec.append(jnp.exp(cs_last - cs_col) * dt[:, h:h + 1])
                e_last.append(jnp.exp(cs_last))
            y_p = jnp.where(low_lane, jnp.dot(ws[0], x_bf, preferred_element_type=F32),
                            jnp.dot(ws[1], x_bf, preferred_element_type=F32))
            y_p = y_p + jnp.where(low_lane, e_col[0], e_col[1]) * lax.dot_general(
                c_g, s_p.astype(BF16), (((1,), (1,)), ((), ())), preferred_element_type=F32)
            ys_ref[:, p * LANE:(p + 1) * LANE] = y_p
            xw = (x_p * jnp.where(low_lane, wdec[0], wdec[1])).astype(BF16)
            upd = lax.dot_general(xw, b_g, (((0,), (0,)), ((), ())), preferred_element_type=F32)
            s_ref[p] = jnp.where(low_row, e_last[0], e_last[1]) * s_p + upd

    z = z_ref[0]
    y = (ys_ref[...] + dsk_ref[...] * xs) * (z * _sigmoid(z))
    gsz = SSM_INNER // SSM_GROUPS
    nw = nw_ref[...]
    for g in range(SSM_GROUPS):
        yg = y[:, g * gsz:(g + 1) * gsz]
        ms = jnp.mean(yg * yg, axis=-1, keepdims=True)
        y_ref[0, :, g * gsz:(g + 1) * gsz] = (yg * lax.rsqrt(ms + EPS) * nw[:, g * gsz:(g + 1) * gsz]).astype(y_ref.dtype)

    @pl.when(c == pl.num_programs(1) - 1)
    def _():
        st_ref[0] = s_ref[...].reshape(SSM_HEADS, SSM_P, SSM_N)


def _pad_lanes(v):
    return jnp.pad(v.astype(F32), (0, LANE - v.shape[0])).reshape(1, LANE)


def _ssd(proj, conv_w, conv_b, dt_bias, a_log, d_skip, ssm_norm_w, conv0, s0):
    bsz, t, _ = proj.shape
    L = min(256, t)
    has_state = s0 is not None

    def const(shape):
        return pl.BlockSpec(shape, lambda b, c: (0,) * len(shape))

    in_specs = [pl.BlockSpec((1, L, SSM_INNER), lambda b, c: (b, c, NB_Z * LANE // SSM_INNER)),
                pl.BlockSpec((1, L, CONV_DIM), lambda b, c: (b, c, NB_XBC * LANE // CONV_DIM)),
                pl.BlockSpec((1, L, LANE), lambda b, c: (b, c, NB_DT)),
                const((CONV_W, CONV_DIM)), const((1, CONV_DIM)), const((1, LANE)), const((1, LANE)),
                const((1, SSM_INNER)), const((1, SSM_INNER))]
    args = [proj, proj, proj, conv_w, conv_b.reshape(1, CONV_DIM), _pad_lanes(dt_bias), _pad_lanes(a_log),
            jnp.repeat(d_skip.astype(F32), SSM_P).reshape(1, SSM_INNER), ssm_norm_w.reshape(1, SSM_INNER)]
    if has_state:
        in_specs += [pl.BlockSpec((1, CONV_W - 1, CONV_DIM), lambda b, c: (b, 0, 0)),
                     pl.BlockSpec((1, SSM_HEADS, SSM_P, SSM_N), lambda b, c: (b, 0, 0, 0))]
        args += [conv0, s0]
    return pl.pallas_call(
        functools.partial(_ssd_kernel, L=L, has_state=has_state),
        grid=(bsz, t // L),
        in_specs=in_specs,
        out_specs=[pl.BlockSpec((1, L, SSM_INNER), lambda b, c: (b, c, 0)),
                   pl.BlockSpec((1, SSM_HEADS, SSM_P, SSM_N), lambda b, c: (b, 0, 0, 0)),
                   pl.BlockSpec((1, CONV_W - 1, CONV_DIM), lambda b, c: (b, 0, 0))],
        out_shape=[jax.ShapeDtypeStruct((bsz, t, SSM_INNER), BF16),
                   jax.ShapeDtypeStruct((bsz, SSM_HEADS, SSM_P, SSM_N), F32),
                   jax.ShapeDtypeStruct((bsz, CONV_W - 1, CONV_DIM), F32)],
        scratch_shapes=[pltpu.VMEM((L + SSD_PAD, CONV_DIM), F32),
                        pltpu.VMEM((SSM_HEADS // 2, 2 * SSM_P, SSM_N), F32),
                        pltpu.VMEM((L, SSM_INNER), F32)],
        compiler_params=pltpu.CompilerParams(
            dimension_semantics=("parallel", "arbitrary"), vmem_limit_bytes=VMEM_LIMIT),
        name="conv_ssd_scan",
    )(*args)


def _neg_tri_ext(kb):
    kk = lax.broadcasted_iota(jnp.int32, (kb, kb + LANE), 0)
    jj = lax.broadcasted_iota(jnp.int32, (kb, kb + LANE), 1)
    return jnp.where((kk >= jj) | (jj >= kb), -1.0, 0.0).astype(BF16)


def _sb_kernel(*refs, QB, KP, n_past):
    if n_past:
        q_ref, k_ref, v_ref, g_ref, kp_ref, vp_ref, o_ref = refs[:7]
        scr = refs[7:]
    else:
        q_ref, k_ref, v_ref, g_ref, o_ref = refs[:5]
        scr = refs[5:]
    nh = SB_HEADS
    acc_refs, r_refs, xn_ref, tn_ref = scr[0:nh], scr[nh:2 * nh], scr[2 * nh:3 * nh], scr[3 * nh:4 * nh]
    if n_past:
        xp_ref, tp_ref = scr[4 * nh:5 * nh], scr[5 * nh:6 * nh]
    qi = pl.program_id(1)
    nt = (((1,), (1,)), ((), ()))

    def lanes(h):
        return slice(h * SB_DH, (h + 1) * SB_DH)

    def qk(h, k_t):
        return lax.dot_general(q_ref[0, :, lanes(h)], k_t, nt, preferred_element_type=F32)

    def scores(h, z, ntri, x_ref, t_ref, mask=None):
        kb = z.shape[1]
        sp = jnp.maximum(z, 0.0) + jnp.log2(1.0 + jnp.exp2(-jnp.abs(z)))
        if mask is not None:
            sp = jnp.where(mask, sp, 0.0)
        wt = jnp.dot(sp.astype(BF16), ntri, preferred_element_type=F32)
        x = z + wt[:, :kb]
        if mask is not None:
            x = jnp.where(mask, x, -jnp.inf)
        x_ref[h][...] = x
        t_ref[h][...] = wt[:, kb:]

    def accumulate(h, v_t, x_ref, t_ref):
        kb = v_t.shape[0]
        r = r_refs[h][...]
        w = jnp.exp2(x_ref[h][...] + jnp.tile(r, (1, max(kb // LANE, 1)))[:, :kb])
        acc_refs[h][...] += jnp.dot(w.astype(BF16), v_t, preferred_element_type=F32)
        r_refs[h][...] = r + t_ref[h][...]

    for h in range(SB_HEADS):
        acc_refs[h][...] = jnp.zeros_like(acc_refs[h])
        r_refs[h][...] = jnp.zeros_like(r_refs[h])

    start = pl.multiple_of(qi * QB, QB)
    mask = lax.broadcasted_iota(jnp.int32, (QB, QB), 1) < lax.broadcasted_iota(jnp.int32, (QB, QB), 0)
    ntri_q = _neg_tri_ext(QB)
    for h in range(SB_HEADS):
        scores(h, qk(h, k_ref[0, pl.ds(start, QB), lanes(h)]), ntri_q, xn_ref, tn_ref, mask=mask)

    if QB % LANE == 0:
        def new_body(jj, carry):
            s = pl.multiple_of((qi - 1 - jj) * QB, QB)
            zs = [qk(h, k_ref[0, pl.ds(s, QB), lanes(h)]) for h in range(SB_HEADS)]
            for h in range(SB_HEADS):
                accumulate(h, v_ref[0, pl.ds(s + QB, QB), lanes(h)], xn_ref, tn_ref)
            for h in range(SB_HEADS):
                scores(h, zs[h], ntri_q, xn_ref, tn_ref)
            return carry

        lax.fori_loop(0, qi, new_body, 0)

    if n_past:
        ntri_p = _neg_tri_ext(KP)
        last = (n_past - 1) * KP
        for h in range(SB_HEADS):
            accumulate(h, v_ref[0, 0:QB, lanes(h)], xn_ref, tn_ref)
            scores(h, qk(h, kp_ref[0, last:last + KP, lanes(h)]), ntri_p, xp_ref, tp_ref)

        def past_body(jj, carry):
            s = pl.multiple_of((n_past - 1 - jj) * KP, KP)
            zs = [qk(h, kp_ref[0, pl.ds(s, KP), lanes(h)]) for h in range(SB_HEADS)]
            for h in range(SB_HEADS):
                accumulate(h, vp_ref[0, pl.ds(s + KP, KP), lanes(h)], xp_ref, tp_ref)
            for h in range(SB_HEADS):
                scores(h, zs[h], ntri_p, xp_ref, tp_ref)
            return carry

        lax.fori_loop(1, n_past, past_body, 0)
        for h in range(SB_HEADS):
            accumulate(h, vp_ref[0, 0:KP, lanes(h)], xp_ref, tp_ref)
    else:
        for h in range(SB_HEADS):
            accumulate(h, v_ref[0, 0:QB, lanes(h)], xn_ref, tn_ref)

    gate = g_ref[0]
    for h in range(SB_HEADS):
        gh = gate[:, lanes(h)]
        o_ref[0, :, lanes(h)] = (acc_refs[h][...] * (gh * _sigmoid(gh))).astype(o_ref.dtype)


def _neg_tri_ext_t(kb):
    jj = lax.broadcasted_iota(jnp.int32, (kb + 16, kb), 0)
    kk = lax.broadcasted_iota(jnp.int32, (kb + 16, kb), 1)
    return jnp.where((kk >= jj) | (jj >= kb), -1.0, 0.0).astype(BF16)


def _sbt_kernel(q_ref, k_ref, v_ref, g_ref, o_ref, *scr, QB):
    nh = SB_HEADS
    acc_refs, r_refs, x_refs, t_refs = scr[0:nh], scr[nh:2 * nh], scr[2 * nh:3 * nh], scr[3 * nh:4 * nh]
    qi = pl.program_id(1)
    nt = (((1,), (1,)), ((), ()))
    tn = (((0,), (0,)), ((), ()))
    ntri = _neg_tri_ext_t(QB)

    def lanes(h):
        return slice(h * SB_DH, (h + 1) * SB_DH)

    def qk(h, k_t):
        return lax.dot_general(k_t, q_ref[0, :, lanes(h)], nt, preferred_element_type=F32)

    def scores(h, z, mask=None):
        neg_abs = lax.bitcast_convert_type(lax.bitcast_convert_type(z, jnp.int32) | jnp.int32(-2 ** 31), F32)
        sp = jnp.maximum(z, 0.0) + jnp.log2(1.0 + jnp.exp2(neg_abs))
        if mask is not None:
            sp = jnp.where(mask, sp, 0.0)
        wt = jnp.dot(ntri, sp.astype(BF16), preferred_element_type=F32)
        x = z + wt[:QB]
        if mask is not None:
            x = jnp.where(mask, x, -jnp.inf)
        x_refs[h][...] = x
        t_refs[h][...] = wt[QB:QB + 8]

    def accumulate(h, v_t):
        r = r_refs[h][...]
        w = jnp.exp2(x_refs[h][...] + r[0:1, :])
        acc_refs[h][...] += lax.dot_general(v_t, w.astype(BF16), tn, preferred_element_type=F32)
        r_refs[h][...] = r + t_refs[h][...]

    for h in range(nh):
        acc_refs[h][...] = jnp.zeros_like(acc_refs[h])
        r_refs[h][...] = jnp.zeros_like(r_refs[h])

    start = pl.multiple_of(qi * QB, QB)
    mask = lax.broadcasted_iota(jnp.int32, (QB, QB), 0) < lax.broadcasted_iota(jnp.int32, (QB, QB), 1)
    for h in range(nh):
        scores(h, qk(h, k_ref[0, pl.ds(start, QB), lanes(h)]), mask=mask)

    def body(jj, carry):
        s = pl.multiple_of((qi - 1 - jj) * QB, QB)
        zs = [qk(h, k_ref[0, pl.ds(s, QB), lanes(h)]) for h in range(nh)]
        for h in range(nh):
            accumulate(h, v_ref[0, pl.ds(s + QB, QB), lanes(h)])
        for h in range(nh):
            scores(h, zs[h])
        return carry

    lax.fori_loop(0, qi, body, 0)
    gate = g_ref[0]
    for h in range(nh):
        accumulate(h, v_ref[0, 0:QB, lanes(h)])
        gh = gate[:, lanes(h)]
        o_ref[0, :, lanes(h)] = (acc_refs[h][...].T * (gh * _sigmoid(gh))).astype(o_ref.dtype)


def _stick_breaking(rest, qkv, k_past, v_past, layer):
    bsz, t, _ = rest.shape
    QB = min(256, t)
    KP = 256
    n_past = 0 if k_past is None else k_past.shape[2] // KP
    assert t % QB == 0 and (t == QB or QB % LANE == 0)

    in_specs = [pl.BlockSpec((1, QB, SB_W), lambda b, i: (b, i, 0)),
                pl.BlockSpec((1, t, SB_W), lambda b, i: (b, 0, 1)),
                pl.BlockSpec((1, t, SB_W), lambda b, i: (b, 0, 2)),
                pl.BlockSpec((1, QB, SB_W), lambda b, i: (b, i, NB_SB_G * LANE // SB_W))]
    args = [qkv, qkv, qkv, rest]
    if n_past:
        assert k_past.shape[2] % KP == 0
        past = pl.BlockSpec((None, 1, k_past.shape[2], SB_W), lambda b, i: (layer, b, 0, 0))
        in_specs += [past, past]
        args += [k_past, v_past]
    if not n_past and QB % LANE == 0:
        return pl.pallas_call(
            functools.partial(_sbt_kernel, QB=QB),
            grid=(bsz, t // QB),
            in_specs=in_specs,
            out_specs=pl.BlockSpec((1, QB, SB_W), lambda b, i: (b, i, 0)),
            out_shape=jax.ShapeDtypeStruct((bsz, t, SB_W), BF16),
            scratch_shapes=[pltpu.VMEM((SB_DH, QB), F32)] * SB_HEADS + [pltpu.VMEM((8, QB), F32)] * SB_HEADS
            + [pltpu.VMEM((QB, QB), F32)] * SB_HEADS + [pltpu.VMEM((8, QB), F32)] * SB_HEADS,
            compiler_params=pltpu.CompilerParams(
                dimension_semantics=("parallel", "arbitrary"), vmem_limit_bytes=VMEM_LIMIT),
            name="stick_breaking_t",
        )(*args)
    return pl.pallas_call(
        functools.partial(_sb_kernel, QB=QB, KP=KP, n_past=n_past),
        grid=(bsz, t // QB),
        in_specs=in_specs,
        out_specs=pl.BlockSpec((1, QB, SB_W), lambda b, i: (b, i, 0)),
        out_shape=jax.ShapeDtypeStruct((bsz, t, SB_W), BF16),
        scratch_shapes=[pltpu.VMEM((QB, SB_DH), F32)] * (2 * SB_HEADS) + [pltpu.VMEM((QB, QB), F32)] * SB_HEADS
        + [pltpu.VMEM((QB, LANE), F32)] * SB_HEADS
        + ([pltpu.VMEM((QB, KP), F32)] * SB_HEADS + [pltpu.VMEM((QB, LANE), F32)] * SB_HEADS if n_past else []),
        compiler_params=pltpu.CompilerParams(
            dimension_semantics=("parallel", "arbitrary"), vmem_limit_bytes=VMEM_LIMIT),
        name="stick_breaking",
    )(*args)


def _outproj_kernel(*refs, final):
    if final:
        x_ref, a_ref, b_ref, c_ref, w_ref, fw_ref, o_ref = refs
    else:
        x_ref, a_ref, b_ref, c_ref, w_ref, o_ref = refs
    y = jnp.dot(a_ref[...], w_ref[0:HG_W, :], preferred_element_type=F32)
    y = y + jnp.dot(b_ref[...], w_ref[HG_W:HG_W + SSM_INNER, :], preferred_element_type=F32)
    y = y + jnp.dot(c_ref[...], w_ref[HG_W + SSM_INNER:, :], preferred_element_type=F32)
    x = x_ref[...] + y
    if final:
        ms = jnp.mean(x * x, axis=-1, keepdims=True)
        x = x * lax.rsqrt(ms + EPS) * fw_ref[...]
    o_ref[...] = x


def _out_proj(x2d, o_hg, y_ssm, o_sb, w_bf, final_w):
    m, d = x2d.shape
    tm = min(1024, m)
    final = final_w is not None

    def rows(w):
        return pl.BlockSpec((tm, w), lambda i: (i, 0))

    in_specs = [rows(d), rows(HG_W), rows(SSM_INNER), rows(SB_W),
                pl.BlockSpec(w_bf.shape, lambda i: (0, 0))]
    args = [x2d, o_hg, y_ssm, o_sb, w_bf]
    if final:
        in_specs.append(pl.BlockSpec((1, d), lambda i: (0, 0)))
        args.append(final_w.reshape(1, d))
    return pl.pallas_call(
        functools.partial(_outproj_kernel, final=final),
        grid=(m // tm,),
        in_specs=in_specs,
        out_specs=rows(d),
        out_shape=jax.ShapeDtypeStruct((m, d), F32),
        compiler_params=pltpu.CompilerParams(
            dimension_semantics=("parallel",), vmem_limit_bytes=VMEM_LIMIT),
        name="out_proj",
    )(*args)


def _reorder_w_in(w_in):
    hg_cols = 4 * HG_W
    ssm_cols = SSM_INNER + CONV_DIM + SSM_HEADS
    hg = w_in[..., :hg_cols]
    z_xbc = w_in[..., hg_cols:hg_cols + SSM_INNER + CONV_DIM]
    dt = w_in[..., hg_cols + SSM_INNER + CONV_DIM:hg_cols + ssm_cols]
    sb = w_in[..., hg_cols + ssm_cols:]
    sb_q = sb[..., :SB_W] * (LOG2E * SB_DH ** -0.5)
    sb_kv = sb[..., SB_W:3 * SB_W]
    sb_g = sb[..., 3 * SB_W:]
    z = z_xbc[..., :SSM_INNER]
    xbc = z_xbc[..., SSM_INNER:]
    used = w_in.shape[-1]
    pad = jnp.zeros(w_in.shape[:-1] + (NP - used,), w_in.dtype)
    return jnp.concatenate([sb_q, sb_kv, sb_g, xbc, dt, pad, z, hg], axis=-1).astype(BF16)


def _lower_bounds(lb_logits):
    p = jax.nn.softmax(lb_logits.astype(F32), axis=0)
    return jnp.clip(jnp.cumsum(p, axis=0) - p[0:1], 0.0, 1.0)


def _layer(x, states, kv_all, layer, depth, lb, norm_w, w_in_bf, conv_w, conv_b, dt_bias, a_log, d_skip,
           ssm_norm_w, hg_norm_w, w_out_bf, final_w):
    bsz, t, d = x.shape
    conv0, ssm0, hg0, k_past, v_past = states
    x2d = x.reshape(bsz * t, d)
    rest, qkv, k_all, v_all = _in_proj(x2d, norm_w, w_in_bf, layer, depth, *kv_all)
    rest = rest.reshape(bsz, t, NR)
    qkv = qkv.reshape(bsz, t, QKV_W)
    o_hg, hg_state = _hgrn(rest, lb, hg_norm_w, hg0)
    y_ssm, ssm_state, conv_state = _ssd(rest, conv_w, conv_b, dt_bias, a_log, d_skip, ssm_norm_w, conv0, ssm0)
    o_sb = _stick_breaking(rest, qkv, k_past, v_past, layer)
    x_new = _out_proj(x2d, o_hg.reshape(bsz * t, HG_W), y_ssm.reshape(bsz * t, SSM_INNER),
                      o_sb.reshape(bsz * t, SB_W), w_out_bf, final_w).reshape(bsz, t, d)
    return x_new, (hg_state, ssm_state, conv_state), (k_all, v_all)


def _run_group(x, layer_states, lbs, w_in_bf, w_out_bf, params, final_norm_w):
    norm_w, conv_w, conv_b, dt_bias, a_log, d_skip, ssm_norm_w, hg_norm_w = params
    depth = norm_w.shape[0]
    bsz, t, _ = x.shape
    outs = []
    kv_all = (jnp.zeros((depth, bsz * t, SB_W), F32), jnp.zeros((depth, bsz * t, SB_W), F32))
    for l in range(depth):
        x, st, kv_all = _layer(x, layer_states(l), kv_all, l, depth, lbs[l], norm_w[l], w_in_bf[l], conv_w[l],
                               conv_b[l], dt_bias[l], a_log[l], d_skip[l], ssm_norm_w[l], hg_norm_w[l],
                               w_out_bf[l], final_norm_w if l == depth - 1 else None)
        outs.append(st)
    kv = tuple(a.reshape(depth, bsz, t, SB_HEADS, SB_DH) for a in kv_all)
    return x, tuple(jnp.stack([st[i] for st in outs]) for i in range(3)) + kv


def kernel(x_prompt, x_sample, state_hgrn, state_ssm, state_conv, cache_k, cache_v, norm_w, w_in, lb_logits,
           conv_w, conv_b, dt_bias, a_log, d_skip, ssm_norm_w, hg_norm_w, w_out, final_norm_w):
    lbs = _lower_bounds(lb_logits)
    w_in_bf = _reorder_w_in(w_in)
    w_out_bf = w_out.astype(BF16)
    params = (norm_w, conv_w, conv_b, dt_bias, a_log, d_skip, ssm_norm_w, hg_norm_w)

    y_prompt, p_st = _run_group(x_prompt, lambda l: (None, None, None, None, None),
                                lbs, w_in_bf, w_out_bf, params, final_norm_w)
    k_cache = cache_k.reshape(cache_k.shape[:3] + (SB_W,)).astype(BF16)
    v_cache = cache_v.reshape(cache_v.shape[:3] + (SB_W,)).astype(BF16)
    y_sample, s_st = _run_group(x_sample,
                                lambda l: (state_conv[l], state_ssm[l], state_hgrn[l], k_cache, v_cache),
                                lbs, w_in_bf, w_out_bf, params, final_norm_w)
    return (y_prompt, y_sample) + p_st + s_st
```

```python
import functools

import jax
import jax.numpy as jnp
from jax import lax
from jax.experimental import pallas as pl
from jax.experimental.pallas import tpu as pltpu

F32 = jnp.float32
BF16 = jnp.bfloat16
EPS = 1e-6

LANE = 128
HG_HEADS = 4
HG_D = 128
SSM_HEADS = 16
SSM_P = 64
SSM_N = 128
SSM_GROUPS = 2
SSM_INNER = SSM_HEADS * SSM_P
CONV_W = 4
CONV_DIM = SSM_INNER + 2 * SSM_GROUPS * SSM_N
SB_HEADS = 4
SB_DH = 128
HG_W = HG_HEADS * HG_D
SB_W = SB_HEADS * SB_DH

PROJ_TN = 8 * LANE
QKV_TILES = 2
QKV_W = QKV_TILES * PROJ_TN
NB_XBC = 0
NB_DT = 12
NB_Z = 16
NB_HG_Q, NB_HG_F, NB_HG_I, NB_HG_G = 24, 28, 32, 36
NB_SB_G = 44
REST_TILES = 6
NR = REST_TILES * PROJ_TN
NP = 7 * PROJ_TN
LOG2E = 1.4426950408889634

VMEM_LIMIT = 56 * 1024 * 1024

SSD_PAD = 8


def _sigmoid(x):
    return 1.0 / (1.0 + jnp.exp(-x))


def _cumsum_rows_scan(x):
    n = x.shape[0]
    row = lax.broadcasted_iota(jnp.int32, x.shape, 0)
    sh = 1
    while sh < n:
        x = x + jnp.where(row >= sh, pltpu.roll(x, sh, axis=0), 0.0)
        sh *= 2
    return x


def _split3(x):
    hi = x.astype(BF16)
    r1 = x - hi.astype(F32)
    mid = r1.astype(BF16)
    lo = (r1 - mid.astype(F32)).astype(BF16)
    return hi, mid, lo


def _cumsum_rows(x):
    n = x.shape[0]
    tri = (lax.broadcasted_iota(jnp.int32, (n, n), 0) >= lax.broadcasted_iota(jnp.int32, (n, n), 1))
    tri = jnp.where(tri, 1.0, 0.0).astype(BF16)
    hi, mid, lo = _split3(x)
    out = jnp.dot(tri, lo, preferred_element_type=F32)
    out = out + jnp.dot(tri, mid, preferred_element_type=F32)
    return out + jnp.dot(tri, hi, preferred_element_type=F32)


def _transpose_rows(x):
    n = x.shape[0]
    if n < LANE:
        x = jnp.concatenate([x, jnp.zeros((LANE - n, LANE), x.dtype)], axis=0)
    return x.T[:, :n]


def _inproj_kernel(x_ref, nw_ref, w_ref, k_in_ref, v_in_ref, rest_ref, qkv_ref, k_ref, v_ref, xn_ref):
    del k_in_ref, v_in_ref
    j = pl.program_id(1)

    @pl.when(j == 0)
    def _():
        x = x_ref[...]
        ms = jnp.mean(x * x, axis=-1, keepdims=True)
        xn_ref[...] = (x * lax.rsqrt(ms + EPS) * nw_ref[...]).astype(BF16)

    acc = jnp.dot(xn_ref[...], w_ref[...], preferred_element_type=F32)

    @pl.when(j < QKV_TILES)
    def _():
        qkv_ref[...] = acc.astype(BF16)

    def put_heads(ref, a):
        for h in range(SB_HEADS):
            a_h = a[:, h * SB_DH:(h + 1) * SB_DH]
            ref[:, :, h, :] = a_h.reshape(ref.shape[0], ref.shape[1], SB_DH)

    @pl.when(j == 0)
    def _():
        put_heads(k_ref, acc[:, SB_W:])

    @pl.when(j == 1)
    def _():
        put_heads(v_ref, acc[:, :SB_W])

    @pl.when(j >= 1)
    def _():
        rest_ref[...] = acc


def _in_proj(x2d, norm_w, w_bf, layer, depth, k_all, v_all, bsz, t):
    m, d = x2d.shape
    tm = min(1024, m)
    in_specs = [
        pl.BlockSpec((tm, d), lambda i, j: (i, 0)),
        pl.BlockSpec((1, d), lambda i, j: (0, 0)),
        pl.BlockSpec((d, PROJ_TN), lambda i, j: (0, j)),
        pl.BlockSpec(memory_space=pl.ANY),
        pl.BlockSpec(memory_space=pl.ANY),
    ]
    args = [x2d, norm_w.reshape(1, d), w_bf, k_all, v_all]
    if t >= tm:
        kv_spec = pl.BlockSpec((None, 1, tm, SB_HEADS, SB_DH),
                               lambda i, j: (layer, i // (t // tm), i % (t // tm), 0, 0))
    else:
        kv_spec = pl.BlockSpec((None, tm // t, t, SB_HEADS, SB_DH), lambda i, j: (layer, i, 0, 0, 0))
    kv_shape = jax.ShapeDtypeStruct((depth, bsz, t, SB_HEADS, SB_DH), F32)
    return pl.pallas_call(
        _inproj_kernel,
        grid=(m // tm, NP // PROJ_TN),
        in_specs=in_specs,
        out_specs=[pl.BlockSpec((tm, PROJ_TN), lambda i, j: (i, jnp.where(j < QKV_TILES, REST_TILES - 1, j - QKV_TILES))),
                   pl.BlockSpec((tm, PROJ_TN), lambda i, j: (i, jnp.minimum(j, QKV_TILES - 1))),
                   kv_spec, kv_spec],
        out_shape=[jax.ShapeDtypeStruct((m, NR), F32),
                   jax.ShapeDtypeStruct((m, QKV_W), BF16),
                   kv_shape, kv_shape],
        input_output_aliases={3: 2, 4: 3},
        scratch_shapes=[pltpu.VMEM((tm, d), BF16)],
        compiler_params=pltpu.CompilerParams(
            dimension_semantics=("parallel", "arbitrary"), vmem_limit_bytes=VMEM_LIMIT),
        name="in_proj",
    )(*args)


def _hgrn_kernel(*refs, C, has_state):
    if has_state:
        q_ref, f_ref, i_ref, g_ref, lb_ref, nw_ref, s0_ref, o_ref, st_ref = refs[:9]
        scr = refs[9:]
    else:
        q_ref, f_ref, i_ref, g_ref, lb_ref, nw_ref, o_ref, st_ref = refs[:8]
        scr = refs[8:]
    sT_refs, b_refs = scr[:HG_HEADS], scr[HG_HEADS:]
    c = pl.program_id(1)

    @pl.when(c == 0)
    def _():
        for h in range(HG_HEADS):
            if has_state:
                sT_refs[h][...] = s0_ref[0, h].T
            else:
                sT_refs[h][...] = jnp.zeros((HG_D, HG_D), F32)

    for h in range(HG_HEADS):
        _hgrn_head(h, q_ref, f_ref, i_ref, g_ref, lb_ref, nw_ref, o_ref, sT_refs[h], b_refs[h], C)

    @pl.when(c == pl.num_programs(1) - 1)
    def _():
        for h in range(HG_HEADS):
            st_ref[0, h] = sT_refs[h][...].T


def _hgrn_head(head, q_ref, f_ref, i_ref, g_ref, lb_ref, nw_ref, o_ref, sT_ref, b_ref, C):
    cols = slice(head * HG_D, (head + 1) * HG_D)
    hq = q_ref[0, :, cols]
    hf = f_ref[0, :, cols]
    v = i_ref[0, :, cols]
    gate = g_ref[0, :, cols]
    lb = lb_ref[:, cols]

    q = hq * _sigmoid(hq)
    e = jnp.exp(-jnp.abs(hf))
    la = jnp.log(lb)
    lc = jnp.log1p(-lb) + (jnp.minimum(hf, 0.0) - jnp.log(1.0 + e))
    log_f = jnp.maximum(la, lc) + jnp.log(1.0 + jnp.exp(-jnp.abs(la - lc)))
    k = (1.0 - lb) * (jnp.where(hf >= 0.0, e, 1.0) / (1.0 + e))

    b = _cumsum_rows_scan(log_f)
    b_ref[...] = b
    row = lax.broadcasted_iota(jnp.int32, (C, HG_D), 0)

    small = [n for n in (8, 4, 2) if n <= C]
    rr = lax.broadcasted_iota(jnp.int32, (C, C), 0)
    cc = lax.broadcasted_iota(jnp.int32, (C, C), 1)
    sel = jnp.concatenate(
        [jnp.where(cc == ((rr & ~(n - 1)) + n // 2 - 1), 1.0, 0.0).astype(BF16) for n in small], axis=0)
    b3 = jnp.concatenate(_split3(b), axis=1)
    g3 = jnp.dot(sel, b3, preferred_element_type=F32)
    b_mid_small = {}
    for i, n in enumerate(small):
        g = g3[i * C:(i + 1) * C]
        b_mid_small[n] = (g[:, 2 * HG_D:] + g[:, HG_D:2 * HG_D]) + g[:, :HG_D]

    att = jnp.zeros((C, C), F32)
    n = C
    while n >= 2:
        h = n // 2
        if n >= 16:
            pieces = [jnp.broadcast_to(b_ref[blk * n + h - 1:blk * n + h, :], (n, HG_D)) for blk in range(C // n)]
            b_mid = pieces[0] if len(pieces) == 1 else jnp.concatenate(pieces, axis=0)
        else:
            b_mid = b_mid_small[n]
        second = (row & (n - 1)) >= h
        ex = jnp.exp(jnp.where(second, b - b_mid, b_mid - b))
        q_n = jnp.where(second, q * ex, 0.0).astype(BF16)
        k_n = jnp.where(second, 0.0, k * ex).astype(BF16)
        a_n = lax.dot_general(q_n, k_n, (((1,), (1,)), ((), ())), preferred_element_type=F32)
        if n < C:
            shift = n.bit_length() - 1
            a_n = jnp.where((rr >> shift) == (cc >> shift), a_n, 0.0)
        att = att + a_n
        n = h

    ones = jnp.ones((HG_D, HG_D), BF16)
    o = jnp.dot((q * k).astype(BF16), ones, preferred_element_type=F32) * v

    s_t = sT_ref[...]
    v_bf = v.astype(BF16)
    o = o + lax.dot_general((q * jnp.exp(b)).astype(BF16), s_t.astype(BF16),
                            (((1,), (1,)), ((), ())), preferred_element_type=F32)
    o = o + jnp.dot(att.astype(BF16), v_bf, preferred_element_type=F32)

    b_last = b_ref[C - 1:C, :]
    k_dec = (k * jnp.exp(b_last - b)).astype(BF16)
    upd = lax.dot_general(v_bf, k_dec, (((0,), (0,)), ((), ())), preferred_element_type=F32)
    sT_ref[...] = s_t * jnp.exp(b_last) + upd

    ms = jnp.mean(o * o, axis=-1, keepdims=True)
    o = o * lax.rsqrt(ms + EPS) * nw_ref[...]
    o_ref[0, :, cols] = (o * (gate * _sigmoid(gate))).astype(o_ref.dtype)


def _hgrn(proj, lb, nw, s0):
    bsz, t, _ = proj.shape
    C = min(128, t)
    has_state = s0 is not None

    def col(nb):
        return pl.BlockSpec((1, C, HG_W), lambda b, c: (b, c, nb // HG_HEADS))

    in_specs = [col(NB_HG_Q), col(NB_HG_F), col(NB_HG_I), col(NB_HG_G),
                pl.BlockSpec((1, HG_W), lambda b, c: (0, 0)),
                pl.BlockSpec((1, HG_D), lambda b, c: (0, 0))]
    args = [proj, proj, proj, proj, lb.reshape(1, HG_W), nw.reshape(1, HG_D)]
    if has_state:
        in_specs.append(pl.BlockSpec((1, HG_HEADS, HG_D, HG_D), lambda b, c: (b, 0, 0, 0)))
        args.append(s0)
    return pl.pallas_call(
        functools.partial(_hgrn_kernel, C=C, has_state=has_state),
        grid=(bsz, t // C),
        in_specs=in_specs,
        out_specs=[pl.BlockSpec((1, C, HG_W), lambda b, c: (b, c, 0)),
                   pl.BlockSpec((1, HG_HEADS, HG_D, HG_D), lambda b, c: (b, 0, 0, 0))],
        out_shape=[jax.ShapeDtypeStruct((bsz, t, HG_W), BF16),
                   jax.ShapeDtypeStruct((bsz, HG_HEADS, HG_D, HG_D), F32)],
        scratch_shapes=[pltpu.VMEM((HG_D, HG_D), F32)] * HG_HEADS + [pltpu.VMEM((C, HG_D), F32)] * HG_HEADS,
        compiler_params=pltpu.CompilerParams(
            dimension_semantics=("parallel", "arbitrary"), vmem_limit_bytes=VMEM_LIMIT),
        name="hgrn2_scan",
    )(*args)


def _ssd_kernel(*refs, L, has_state):
    if has_state:
        (z_ref, xbc_ref, dt_ref, cw_ref, cb_ref, dtb_ref, al_ref, dsk_ref, nw_ref, conv0_ref, s0_ref,
         y_ref, st_ref, cst_ref, xp_ref, s_ref, ys_ref) = refs
    else:
        (z_ref, xbc_ref, dt_ref, cw_ref, cb_ref, dtb_ref, al_ref, dsk_ref, nw_ref,
         y_ref, st_ref, cst_ref, xp_ref, s_ref, ys_ref) = refs
    c = pl.program_id(1)

    @pl.when(c == 0)
    def _():
        xp_ref[0:SSD_PAD, :] = jnp.zeros((SSD_PAD, CONV_DIM), F32)
        if has_state:
            xp_ref[SSD_PAD - (CONV_W - 1):SSD_PAD, :] = conv0_ref[0]
            s_ref[...] = s0_ref[0].reshape(SSM_HEADS // 2, 2 * SSM_P, SSM_N)
        else:
            s_ref[...] = jnp.zeros((SSM_HEADS // 2, 2 * SSM_P, SSM_N), F32)

    xp_ref[SSD_PAD:SSD_PAD + L, :] = xbc_ref[0]
    cw = cw_ref[...]
    conv = cb_ref[...]
    for j in range(CONV_W):
        lo = SSD_PAD - (CONV_W - 1) + j
        conv = conv + cw[j:j + 1, :] * xp_ref[lo:lo + L, :]
    cst_ref[0] = xp_ref[L + SSD_PAD - (CONV_W - 1):L + SSD_PAD, :]
    xp_ref[0:SSD_PAD, :] = xp_ref[L:L + SSD_PAD, :]
    act = conv * _sigmoid(conv)
    xs = act[:, :SSM_INNER]

    dt_raw = dt_ref[0] + dtb_ref[...]
    dt = jnp.maximum(dt_raw, 0.0) + jnp.log(1.0 + jnp.exp(-jnp.abs(dt_raw)))
    a = -jnp.exp(al_ref[...])
    cs = _cumsum_rows(dt * a)
    cs_t = _transpose_rows(cs)
    cs2 = cs * LOG2E
    row2 = _transpose_rows(cs2 - jnp.log2(dt))

    tril = lax.broadcasted_iota(jnp.int32, (L, L), 0) >= lax.broadcasted_iota(jnp.int32, (L, L), 1)
    low_lane = lax.broadcasted_iota(jnp.int32, (L, LANE), 1) < SSM_P
    low_row = lax.broadcasted_iota(jnp.int32, (LANE, SSM_N), 0) < SSM_P
    hpg = SSM_HEADS // SSM_GROUPS
    for g in range(SSM_GROUPS):
        b_g = act[:, SSM_INNER + g * SSM_N:SSM_INNER + (g + 1) * SSM_N].astype(BF16)
        c_g = act[:, SSM_INNER + (SSM_GROUPS + g) * SSM_N:SSM_INNER + (SSM_GROUPS + g + 1) * SSM_N].astype(BF16)
        cb = lax.dot_general(c_g, b_g, (((1,), (1,)), ((), ())), preferred_element_type=F32)
        for pp in range(hpg // 2):
            p = (g * hpg) // 2 + pp
            x_p = xs[:, p * LANE:(p + 1) * LANE]
            x_bf = x_p.astype(BF16)
            s_p = s_ref[p]
            ws, e_col, wdec, e_last = [], [], [], []
            for h in (2 * p, 2 * p + 1):
                cs_col = cs[:, h:h + 1]
                cs_row = cs_t[h:h + 1, :]
                decay = jnp.exp2(jnp.where(tril, cs2[:, h:h + 1] - row2[h:h + 1, :], -jnp.inf))
                ws.append((cb * decay).astype(BF16))
                cs_last = cs_t[h:h + 1, L - 1:L]
                e_col.append(jnp.exp(cs_col))
                wdec.append(jnp.exp(cs_last - cs_col) * dt[:, h:h + 1])
                e_last.append(jnp.exp(cs_last))
            y_p = jnp.where(low_lane, jnp.dot(ws[0], x_bf, preferred_element_type=F32),
                            jnp.dot(ws[1], x_bf, preferred_element_type=F32))
            y_p = y_p + jnp.where(low_lane, e_col[0], e_col[1]) * lax.dot_general(
                c_g, s_p.astype(BF16), (((1,), (1,)), ((), ())), preferred_element_type=F32)
            ys_ref[:, p * LANE:(p + 1) * LANE] = y_p
            xw = (x_p * jnp.where(low_lane, wdec[0], wdec[1])).astype(BF16)
            upd = lax.dot_general(xw, b_g, (((0,), (0,)), ((), ())), preferred_element_type=F32)
            s_ref[p] = jnp.where(low_row, e_last[0], e_last[1]) * s_p + upd

    z = z_ref[0]
    y = (ys_ref[...] + dsk_ref[...] * xs) * (z * _sigmoid(z))
    gsz = SSM_INNER // SSM_GROUPS
    nw = nw_ref[...]
    for g in range(SSM_GROUPS):
        yg = y[:, g * gsz:(g + 1) * gsz]
        ms = jnp.mean(yg * yg, axis=-1, keepdims=True)
        y_ref[0, :, g * gsz:(g + 1) * gsz] = (yg * lax.rsqrt(ms + EPS) * nw[:, g * gsz:(g + 1) * gsz]).astype(y_ref.dtype)

    @pl.when(c == pl.num_programs(1) - 1)
    def _():
        st_ref[0] = s_ref[...].reshape(SSM_HEADS, SSM_P, SSM_N)


def _pad_lanes(v):
    return jnp.pad(v.astype(F32), (0, LANE - v.shape[0])).reshape(1, LANE)


def _ssd(proj, conv_w, conv_b, dt_bias, a_log, d_skip, ssm_norm_w, conv0, s0):
    bsz, t, _ = proj.shape
    L = min(256, t)
    has_state = s0 is not None

    def const(shape):
        return pl.BlockSpec(shape, lambda b, c: (0,) * len(shape))

    in_specs = [pl.BlockSpec((1, L, SSM_INNER), lambda b, c: (b, c, NB_Z * LANE // SSM_INNER)),
                pl.BlockSpec((1, L, CONV_DIM), lambda b, c: (b, c, NB_XBC * LANE // CONV_DIM)),
                pl.BlockSpec((1, L, LANE), lambda b, c: (b, c, NB_DT)),
                const((CONV_W, CONV_DIM)), const((1, CONV_DIM)), const((1, LANE)), const((1, LANE)),
                const((1, SSM_INNER)), const((1, SSM_INNER))]
    args = [proj, proj, proj, conv_w, conv_b.reshape(1, CONV_DIM), _pad_lanes(dt_bias), _pad_lanes(a_log),
            jnp.repeat(d_skip.astype(F32), SSM_P).reshape(1, SSM_INNER), ssm_norm_w.reshape(1, SSM_INNER)]
    if has_state:
        in_specs += [pl.BlockSpec((1, CONV_W - 1, CONV_DIM), lambda b, c: (b, 0, 0)),
                     pl.BlockSpec((1, SSM_HEADS, SSM_P, SSM_N), lambda b, c: (b, 0, 0, 0))]
        args += [conv0, s0]
    return pl.pallas_call(
        functools.partial(_ssd_kernel, L=L, has_state=has_state),
        grid=(bsz, t // L),
        in_specs=in_specs,
        out_specs=[pl.BlockSpec((1, L, SSM_INNER), lambda b, c: (b, c, 0)),
                   pl.BlockSpec((1, SSM_HEADS, SSM_P, SSM_N), lambda b, c: (b, 0, 0, 0)),
                   pl.BlockSpec((1, CONV_W - 1, CONV_DIM), lambda b, c: (b, 0, 0))],
        out_shape=[jax.ShapeDtypeStruct((bsz, t, SSM_INNER), BF16),
                   jax.ShapeDtypeStruct((bsz, SSM_HEADS, SSM_P, SSM_N), F32),
                   jax.ShapeDtypeStruct((bsz, CONV_W - 1, CONV_DIM), F32)],
        scratch_shapes=[pltpu.VMEM((L + SSD_PAD, CONV_DIM), F32),
                        pltpu.VMEM((SSM_HEADS // 2, 2 * SSM_P, SSM_N), F32),
                        pltpu.VMEM((L, SSM_INNER), F32)],
        compiler_params=pltpu.CompilerParams(
            dimension_semantics=("parallel", "arbitrary"), vmem_limit_bytes=VMEM_LIMIT),
        name="conv_ssd_scan",
    )(*args)


def _neg_tri_ext(kb):
    kk = lax.broadcasted_iota(jnp.int32, (kb, kb + LANE), 0)
    jj = lax.broadcasted_iota(jnp.int32, (kb, kb + LANE), 1)
    return jnp.where((kk >= jj) | (jj >= kb), -1.0, 0.0).astype(BF16)


def _sb_kernel(*refs, QB, KP, n_past):
    if n_past:
        q_ref, k_ref, v_ref, g_ref, kp_ref, vp_ref, o_ref = refs[:7]
        scr = refs[7:]
    else:
        q_ref, k_ref, v_ref, g_ref, o_ref = refs[:5]
        scr = refs[5:]
    nh = SB_HEADS
    acc_refs, r_refs, xn_ref, tn_ref = scr[0:nh], scr[nh:2 * nh], scr[2 * nh:3 * nh], scr[3 * nh:4 * nh]
    if n_past:
        xp_ref, tp_ref = scr[4 * nh:5 * nh], scr[5 * nh:6 * nh]
    qi = pl.program_id(1)
    nt = (((1,), (1,)), ((), ()))

    def lanes(h):
        return slice(h * SB_DH, (h + 1) * SB_DH)

    def qk(h, k_t):
        return lax.dot_general(q_ref[0, :, lanes(h)], k_t, nt, preferred_element_type=F32)

    def scores(h, z, ntri, x_ref, t_ref, mask=None):
        kb = z.shape[1]
        sp = jnp.maximum(z, 0.0) + jnp.log2(1.0 + jnp.exp2(-jnp.abs(z)))
        if mask is not None:
            sp = jnp.where(mask, sp, 0.0)
        wt = jnp.dot(sp.astype(BF16), ntri, preferred_element_type=F32)
        x = z + wt[:, :kb]
        if mask is not None:
            x = jnp.where(mask, x, -jnp.inf)
        x_ref[h][...] = x
        t_ref[h][...] = wt[:, kb:]

    def accumulate(h, v_t, x_ref, t_ref):
        kb = v_t.shape[0]
        r = r_refs[h][...]
        w = jnp.exp2(x_ref[h][...] + jnp.tile(r, (1, max(kb // LANE, 1)))[:, :kb])
        acc_refs[h][...] += jnp.dot(w.astype(BF16), v_t, preferred_element_type=F32)
        r_refs[h][...] = r + t_ref[h][...]

    for h in range(SB_HEADS):
        acc_refs[h][...] = jnp.zeros_like(acc_refs[h])
        r_refs[h][...] = jnp.zeros_like(r_refs[h])

    start = pl.multiple_of(qi * QB, QB)
    mask = lax.broadcasted_iota(jnp.int32, (QB, QB), 1) < lax.broadcasted_iota(jnp.int32, (QB, QB), 0)
    ntri_q = _neg_tri_ext(QB)
    for h in range(SB_HEADS):
        scores(h, qk(h, k_ref[0, pl.ds(start, QB), lanes(h)]), ntri_q, xn_ref, tn_ref, mask=mask)

    if QB % LANE == 0:
        def new_body(jj, carry):
            s = pl.multiple_of((qi - 1 - jj) * QB, QB)
            zs = [qk(h, k_ref[0, pl.ds(s, QB), lanes(h)]) for h in range(SB_HEADS)]
            for h in range(SB_HEADS):
                accumulate(h, v_ref[0, pl.ds(s + QB, QB), lanes(h)], xn_ref, tn_ref)
            for h in range(SB_HEADS):
                scores(h, zs[h], ntri_q, xn_ref, tn_ref)
            return carry

        lax.fori_loop(0, qi, new_body, 0)

    if n_past:
        ntri_p = _neg_tri_ext(KP)
        last = (n_past - 1) * KP
        for h in range(SB_HEADS):
            accumulate(h, v_ref[0, 0:QB, lanes(h)], xn_ref, tn_ref)
            scores(h, qk(h, kp_ref[0, last:last + KP, lanes(h)]), ntri_p, xp_ref, tp_ref)

        def past_body(jj, carry):
            s = pl.multiple_of((n_past - 1 - jj) * KP, KP)
            zs = [qk(h, kp_ref[0, pl.ds(s, KP), lanes(h)]) for h in range(SB_HEADS)]
            for h in range(SB_HEADS):
                accumulate(h, vp_ref[0, pl.ds(s + KP, KP), lanes(h)], xp_ref, tp_ref)
            for h in range(SB_HEADS):
                scores(h, zs[h], ntri_p, xp_ref, tp_ref)
            return carry

        lax.fori_loop(1, n_past, past_body, 0)
        for h in range(SB_HEADS):
            accumulate(h, vp_ref[0, 0:KP, lanes(h)], xp_ref, tp_ref)
    else:
        for h in range(SB_HEADS):
            accumulate(h, v_ref[0, 0:QB, lanes(h)], xn_ref, tn_ref)

    gate = g_ref[0]
    for h in range(SB_HEADS):
        gh = gate[:, lanes(h)]
        o_ref[0, :, lanes(h)] = (acc_refs[h][...] * (gh * _sigmoid(gh))).astype(o_ref.dtype)


def _neg_tri_ext_t(kb):
    jj = lax.broadcasted_iota(jnp.int32, (kb + 16, kb), 0)
    kk = lax.broadcasted_iota(jnp.int32, (kb + 16, kb), 1)
    return jnp.where((kk >= jj) | (jj >= kb), -1.0, 0.0).astype(BF16)


def _sbt_kernel(q_ref, k_ref, v_ref, g_ref, o_ref, *scr, QB):
    nh = SB_HEADS
    acc_refs, r_refs, x_refs, t_refs = scr[0:nh], scr[nh:2 * nh], scr[2 * nh:3 * nh], scr[3 * nh:4 * nh]
    qi = pl.program_id(1)
    nt = (((1,), (1,)), ((), ()))
    tn = (((0,), (0,)), ((), ()))
    ntri = _neg_tri_ext_t(QB)

    def lanes(h):
        return slice(h * SB_DH, (h + 1) * SB_DH)

    def qk(h, k_t):
        return lax.dot_general(k_t, q_ref[0, :, lanes(h)], nt, preferred_element_type=F32)

    def scores(h, z, mask=None):
        neg_abs = lax.bitcast_convert_type(lax.bitcast_convert_type(z, jnp.int32) | jnp.int32(-2 ** 31), F32)
        sp = jnp.maximum(z, 0.0) + jnp.log2(1.0 + jnp.exp2(neg_abs))
        if mask is not None:
            sp = jnp.where(mask, sp, 0.0)
        wt = jnp.dot(ntri, sp.astype(BF16), preferred_element_type=F32)
        x = z + wt[:QB]
        if mask is not None:
            x = jnp.where(mask, x, -jnp.inf)
        x_refs[h][...] = x
        t_refs[h][...] = wt[QB:QB + 8]

    def accumulate(h, v_t):
        r = r_refs[h][...]
        w = jnp.exp2(x_refs[h][...] + r[0:1, :])
        acc_refs[h][...] += lax.dot_general(v_t, w.astype(BF16), tn, preferred_element_type=F32)
        r_refs[h][...] = r + t_refs[h][...]

    for h in range(nh):
        acc_refs[h][...] = jnp.zeros_like(acc_refs[h])
        r_refs[h][...] = jnp.zeros_like(r_refs[h])

    start = pl.multiple_of(qi * QB, QB)
    mask = lax.broadcasted_iota(jnp.int32, (QB, QB), 0) < lax.broadcasted_iota(jnp.int32, (QB, QB), 1)
    for h in range(nh):
        scores(h, qk(h, k_ref[0, pl.ds(start, QB), lanes(h)]), mask=mask)

    def body(jj, carry):
        s = pl.multiple_of((qi - 1 - jj) * QB, QB)
        zs = [qk(h, k_ref[0, pl.ds(s, QB), lanes(h)]) for h in range(nh)]
        for h in range(nh):
            accumulate(h, v_ref[0, pl.ds(s + QB, QB), lanes(h)])
        for h in range(nh):
            scores(h, zs[h])
        return carry

    lax.fori_loop(0, qi, body, 0)
    gate = g_ref[0]
    for h in range(nh):
        accumulate(h, v_ref[0, 0:QB, lanes(h)])
        gh = gate[:, lanes(h)]
        o_ref[0, :, lanes(h)] = (acc_refs[h][...].T * (gh * _sigmoid(gh))).astype(o_ref.dtype)


def _stick_breaking(rest, qkv, k_past, v_past, layer):
    bsz, t, _ = rest.shape
    QB = min(256, t)
    KP = 256
    n_past = 0 if k_past is None else k_past.shape[2] // KP
    assert t % QB == 0 and (t == QB or QB % LANE == 0)

    in_specs = [pl.BlockSpec((1, QB, SB_W), lambda b, i: (b, i, 0)),
                pl.BlockSpec((1, t, SB_W), lambda b, i: (b, 0, 1)),
                pl.BlockSpec((1, t, SB_W), lambda b, i: (b, 0, 2)),
                pl.BlockSpec((1, QB, SB_W), lambda b, i: (b, i, NB_SB_G * LANE // SB_W))]
    args = [qkv, qkv, qkv, rest]
    if n_past:
        assert k_past.shape[2] % KP == 0
        past = pl.BlockSpec((None, 1, k_past.shape[2], SB_W), lambda b, i: (layer, b, 0, 0))
        in_specs += [past, past]
        args += [k_past, v_past]
    if not n_past and QB % LANE == 0:
        return pl.pallas_call(
            functools.partial(_sbt_kernel, QB=QB),
            grid=(bsz, t // QB),
            in_specs=in_specs,
            out_specs=pl.BlockSpec((1, QB, SB_W), lambda b, i: (b, i, 0)),
            out_shape=jax.ShapeDtypeStruct((bsz, t, SB_W), BF16),
            scratch_shapes=[pltpu.VMEM((SB_DH, QB), F32)] * SB_HEADS + [pltpu.VMEM((8, QB), F32)] * SB_HEADS
            + [pltpu.VMEM((QB, QB), F32)] * SB_HEADS + [pltpu.VMEM((8, QB), F32)] * SB_HEADS,
            compiler_params=pltpu.CompilerParams(
                dimension_semantics=("parallel", "arbitrary"), vmem_limit_bytes=VMEM_LIMIT),
            name="stick_breaking_t",
        )(*args)
    return pl.pallas_call(
        functools.partial(_sb_kernel, QB=QB, KP=KP, n_past=n_past),
        grid=(bsz, t // QB),
        in_specs=in_specs,
        out_specs=pl.BlockSpec((1, QB, SB_W), lambda b, i: (b, i, 0)),
        out_shape=jax.ShapeDtypeStruct((bsz, t, SB_W), BF16),
        scratch_shapes=[pltpu.VMEM((QB, SB_DH), F32)] * (2 * SB_HEADS) + [pltpu.VMEM((QB, QB), F32)] * SB_HEADS
        + [pltpu.VMEM((QB, LANE), F32)] * SB_HEADS
        + ([pltpu.VMEM((QB, KP), F32)] * SB_HEADS + [pltpu.VMEM((QB, LANE), F32)] * SB_HEADS if n_past else []),
        compiler_params=pltpu.CompilerParams(
            dimension_semantics=("parallel", "arbitrary"), vmem_limit_bytes=VMEM_LIMIT),
        name="stick_breaking",
    )(*args)


def _outproj_kernel(*refs, final):
    if final:
        x_ref, a_ref, b_ref, c_ref, w_ref, fw_ref, o_ref = refs
    else:
        x_ref, a_ref, b_ref, c_ref, w_ref, o_ref = refs
    y = jnp.dot(a_ref[...], w_ref[0:HG_W, :], preferred_element_type=F32)
    y = y + jnp.dot(b_ref[...], w_ref[HG_W:HG_W + SSM_INNER, :], preferred_element_type=F32)
    y = y + jnp.dot(c_ref[...], w_ref[HG_W + SSM_INNER:, :], preferred_element_type=F32)
    x = x_ref[...] + y
    if final:
        ms = jnp.mean(x * x, axis=-1, keepdims=True)
        x = x * lax.rsqrt(ms + EPS) * fw_ref[...]
    o_ref[...] = x


def _out_proj(x2d, o_hg, y_ssm, o_sb, w_bf, final_w):
    m, d = x2d.shape
    tm = min(1024, m)
    final = final_w is not None

    def rows(w):
        return pl.BlockSpec((tm, w), lambda i: (i, 0))

    in_specs = [rows(d), rows(HG_W), rows(SSM_INNER), rows(SB_W),
                pl.BlockSpec(w_bf.shape, lambda i: (0, 0))]
    args = [x2d, o_hg, y_ssm, o_sb, w_bf]
    if final:
        in_specs.append(pl.BlockSpec((1, d), lambda i: (0, 0)))
        args.append(final_w.reshape(1, d))
    return pl.pallas_call(
        functools.partial(_outproj_kernel, final=final),
        grid=(m // tm,),
        in_specs=in_specs,
        out_specs=rows(d),
        out_shape=jax.ShapeDtypeStruct((m, d), F32),
        compiler_params=pltpu.CompilerParams(
            dimension_semantics=("parallel",), vmem_limit_bytes=VMEM_LIMIT),
        name="out_proj",
    )(*args)


def _reorder_w_in(w_in):
    hg_cols = 4 * HG_W
    ssm_cols = SSM_INNER + CONV_DIM + SSM_HEADS
    hg = w_in[..., :hg_cols]
    z_xbc = w_in[..., hg_cols:hg_cols + SSM_INNER + CONV_DIM]
    dt = w_in[..., hg_cols + SSM_INNER + CONV_DIM:hg_cols + ssm_cols]
    sb = w_in[..., hg_cols + ssm_cols:]
    sb_q = sb[..., :SB_W] * (LOG2E * SB_DH ** -0.5)
    sb_kv = sb[..., SB_W:3 * SB_W]
    sb_g = sb[..., 3 * SB_W:]
    z = z_xbc[..., :SSM_INNER]
    xbc = z_xbc[..., SSM_INNER:]
    used = w_in.shape[-1]
    pad = jnp.zeros(w_in.shape[:-1] + (NP - used,), w_in.dtype)
    return jnp.concatenate([sb_q, sb_kv, sb_g, xbc, dt, pad, z, hg], axis=-1).astype(BF16)


def _lower_bounds(lb_logits):
    p = jax.nn.softmax(lb_logits.astype(F32), axis=0)
    return jnp.clip(jnp.cumsum(p, axis=0) - p[0:1], 0.0, 1.0)


def _layer(x, states, kv_all, layer, depth, lb, norm_w, w_in_bf, conv_w, conv_b, dt_bias, a_log, d_skip,
           ssm_norm_w, hg_norm_w, w_out_bf, final_w):
    bsz, t, d = x.shape
    conv0, ssm0, hg0, k_past, v_past = states
    x2d = x.reshape(bsz * t, d)
    rest, qkv, k_all, v_all = _in_proj(x2d, norm_w, w_in_bf, layer, depth, *kv_all, bsz, t)
    rest = rest.reshape(bsz, t, NR)
    qkv = qkv.reshape(bsz, t, QKV_W)
    o_hg, hg_state = _hgrn(rest, lb, hg_norm_w, hg0)
    y_ssm, ssm_state, conv_state = _ssd(rest, conv_w, conv_b, dt_bias, a_log, d_skip, ssm_norm_w, conv0, ssm0)
    o_sb = _stick_breaking(rest, qkv, k_past, v_past, layer)
    x_new = _out_proj(x2d, o_hg.reshape(bsz * t, HG_W), y_ssm.reshape(bsz * t, SSM_INNER),
                      o_sb.reshape(bsz * t, SB_W), w_out_bf, final_w).reshape(bsz, t, d)
    return x_new, (hg_state, ssm_state, conv_state), (k_all, v_all)


def _run_group(x, layer_states, lbs, w_in_bf, w_out_bf, params, final_norm_w):
    norm_w, conv_w, conv_b, dt_bias, a_log, d_skip, ssm_norm_w, hg_norm_w = params
    depth = norm_w.shape[0]
    bsz, t, _ = x.shape
    outs = []
    kv_all = (jnp.zeros((depth, bsz, t, SB_HEADS, SB_DH), F32), jnp.zeros((depth, bsz, t, SB_HEADS, SB_DH), F32))
    for l in range(depth):
        x, st, kv_all = _layer(x, layer_states(l), kv_all, l, depth, lbs[l], norm_w[l], w_in_bf[l], conv_w[l],
                               conv_b[l], dt_bias[l], a_log[l], d_skip[l], ssm_norm_w[l], hg_norm_w[l],
                               w_out_bf[l], final_norm_w if l == depth - 1 else None)
        outs.append(st)
    return x, tuple(jnp.stack([st[i] for st in outs]) for i in range(3)) + tuple(kv_all)


def kernel(x_prompt, x_sample, state_hgrn, state_ssm, state_conv, cache_k, cache_v, norm_w, w_in, lb_logits,
           conv_w, conv_b, dt_bias, a_log, d_skip, ssm_norm_w, hg_norm_w, w_out, final_norm_w):
    lbs = _lower_bounds(lb_logits)
    w_in_bf = _reorder_w_in(w_in)
    w_out_bf = w_out.astype(BF16)
    params = (norm_w, conv_w, conv_b, dt_bias, a_log, d_skip, ssm_norm_w, hg_norm_w)

    y_prompt, p_st = _run_group(x_prompt, lambda l: (None, None, None, None, None),
                                lbs, w_in_bf, w_out_bf, params, final_norm_w)
    k_cache = cache_k.astype(BF16).reshape(cache_k.shape[:3] + (SB_W,))
    v_cache = cache_v.astype(BF16).reshape(cache_v.shape[:3] + (SB_W,))
    y_sample, s_st = _run_group(x_sample,
                                lambda l: (state_conv[l], state_ssm[l], state_hgrn[l], k_cache, v_cache),
                                lbs, w_in_bf, w_out_bf, params, final_norm_w)
    return (y_prompt, y_sample) + p_st + s_st
```
